```python
import jax, jax.numpy as jnp
from jax import lax
import numpy as np

D_MODEL = 1024
BATCH = 16
SEQ = 4096
DEPTH = 1

N_HEADS_ATTN = 8
HEAD_DIM = 64
ATTN_WIDTH = N_HEADS_ATTN * HEAD_DIM
IDX_HEADS = 8
IDX_DIM = 32
TOPK_MAX = 256
Q_BLOCK = 128
CONV_WIDTH = D_MODEL - ATTN_WIDTH
CONV_GROUPS = 8
CONV_K = 3
MIX_WIDTH = ATTN_WIDTH + CONV_WIDTH
ROPE_THETA = 500000.0
ROPE_DIM = HEAD_DIM // 4
D_FF = ((-(-8 * D_MODEL // 3)) + 255) // 256 * 256
DEEPNORM_ALPHA = (2.0 * DEPTH) ** 0.25
DEEPNORM_BETA = (8.0 * DEPTH) ** -0.25
EPS = 1e-5
NEG_INF = -1e30

IN_SPLITS = (ATTN_WIDTH, HEAD_DIM, HEAD_DIM, IDX_HEADS * IDX_DIM, IDX_DIM, IDX_HEADS,
             CONV_WIDTH, CONV_WIDTH, CONV_WIDTH)
IN_COLS = sum(IN_SPLITS)

kernel_name = "hymba_dsa_shortconv_deepnorm_adaln"


def rope_tables(seq_len):
    pos = jnp.arange(seq_len, dtype=jnp.float32)
    inv = ROPE_THETA ** (-(jnp.arange(0, ROPE_DIM, 2, dtype=jnp.float32) / ROPE_DIM))
    ang = pos[:, None] * inv[None, :]
    return jnp.cos(ang)[:, None, :], jnp.sin(ang)[:, None, :]


def partial_rope(x, cos, sin):
    cos = cos.astype(x.dtype)
    sin = sin.astype(x.dtype)
    half = ROPE_DIM // 2
    x1, x2, rest = x[..., :half], x[..., half:ROPE_DIM], x[..., ROPE_DIM:]
    return jnp.concatenate([x1 * cos - x2 * sin, x2 * cos + x1 * sin, rest], axis=-1)


def layer_norm(x, g, b):
    xf = x.astype(jnp.float32)
    mu = jnp.mean(xf, axis=-1, keepdims=True)
    var = jnp.mean(jnp.square(xf - mu), axis=-1, keepdims=True)
    y = (xf - mu) * lax.rsqrt(var + EPS) * g.astype(jnp.float32) + b.astype(jnp.float32)
    return y.astype(x.dtype)


def group_rms_norm(x, g, n_groups):
    bsz, s, w = x.shape
    xf = x.astype(jnp.float32).reshape(bsz, s, n_groups, w // n_groups)
    xf = xf * lax.rsqrt(jnp.mean(jnp.square(xf), axis=-1, keepdims=True) + EPS)
    return (xf.reshape(bsz, s, w) * g.astype(jnp.float32)).astype(x.dtype)


def modulate(x, shift, scale):
    return x * (1.0 + scale[:, None, :]) + shift[:, None, :]


def sparse_indexer_attention(q, k, v, qi, ki, wi, k_sel):
    bsz, s, h, dh = q.shape
    n_blk = s // Q_BLOCK
    attn_scale = HEAD_DIM ** -0.5
    key_pos = jnp.arange(s)
    ki32 = ki.astype(jnp.float32)

    def to_blocks(a):
        return jnp.moveaxis(a.reshape((bsz, n_blk, Q_BLOCK) + a.shape[2:]), 1, 0)

    def block_fn(args):
        q_b, qi_b, w_b, t_b = args
        rel = jax.nn.relu(jnp.einsum('bqhd,bsd->bqhs', qi_b.astype(jnp.float32), ki32))
        score = jnp.einsum('bqhs,bqh->bqs', rel, w_b.astype(jnp.float32))
        causal = key_pos[None, :] <= t_b[:, None]
        score = jnp.where(causal[None], score, NEG_INF)
        _, sel = lax.top_k(score, k_sel)
        valid = sel <= t_b[None, :, None]
        kg = jax.vmap(lambda kb, ib: kb[ib])(k, sel)
        vg = jax.vmap(lambda vb, ib: vb[ib])(v, sel)
        logits = jnp.einsum('bqhd,bqkd->bhqk', q_b, kg).astype(jnp.float32) * attn_scale
        logits = jnp.where(valid[:, None], logits, NEG_INF)
        p = jax.nn.softmax(logits, axis=-1).astype(v.dtype)
        return jnp.einsum('bhqk,bqkd->bqhd', p, vg)

    t_blocks = key_pos.reshape(n_blk, Q_BLOCK)
    out = lax.map(block_fn, (to_blocks(q), to_blocks(qi), to_blocks(wi), t_blocks))
    return jnp.moveaxis(out, 0, 1).reshape(bsz, s, h, dh)


def token_mixer(h, w_in, conv_w, attn_norm_g, conv_norm_g, w_out, cos, sin):
    bsz, s, _ = h.shape
    k_sel = min(TOPK_MAX, s // 4)
    proj = h @ w_in
    offsets = np.cumsum(IN_SPLITS)[:-1].tolist()
    q, k, v, qi, ki, wi, gate_b, gate_c, hc = jnp.split(proj, offsets, axis=-1)

    q = partial_rope(q.reshape(bsz, s, N_HEADS_ATTN, HEAD_DIM), cos, sin)
    k = partial_rope(k[:, :, None, :], cos, sin)[:, :, 0, :]
    qi = partial_rope(qi.reshape(bsz, s, IDX_HEADS, IDX_DIM), cos, sin)
    ki = partial_rope(ki[:, :, None, :], cos, sin)[:, :, 0, :]
    wi = wi * (IDX_HEADS ** -0.5 * IDX_DIM ** -0.5)
    attn = sparse_indexer_attention(q, k, v, qi, ki, wi, k_sel).reshape(bsz, s, ATTN_WIDTH)
    attn = group_rms_norm(attn, attn_norm_g, N_HEADS_ATTN)

    u = gate_c * hc
    u_pad = jnp.pad(u, ((0, 0), (CONV_K - 1, 0), (0, 0)))
    conv = sum(conv_w[j] * u_pad[:, j:j + s, :] for j in range(CONV_K))
    conv_out = group_rms_norm(gate_b * conv, conv_norm_g, CONV_GROUPS)

    mixed = jnp.concatenate([attn, conv_out], axis=-1)
    return mixed @ w_out


def swiglu(h, w_gate, w_up, w_down):
    return (jax.nn.silu(h @ w_gate) * (h @ w_up)) @ w_down


def setup_inputs(seed: int = 0) -> dict:
    key = jax.random.key(seed)
    ks = jax.random.split(key, 17)
    n = jax.random.normal
    f32 = jnp.float32
    L = DEPTH
    return {
        "x": n(ks[0], (BATCH, SEQ, D_MODEL), f32),
        "c": n(ks[1], (BATCH, D_MODEL), f32),
        "w_ada": n(ks[2], (L, D_MODEL, 6 * D_MODEL), f32) * (D_MODEL ** -0.5),
        "b_ada": n(ks[3], (L, 6 * D_MODEL), f32) * 0.02,
        "w_in": n(ks[4], (L, D_MODEL, IN_COLS), f32) * (D_MODEL ** -0.5),
        "conv_w": n(ks[5], (L, CONV_K, CONV_WIDTH), f32) * (CONV_K ** -0.5),
        "attn_norm_g": 1.0 + 0.1 * n(ks[6], (L, ATTN_WIDTH), f32),
        "conv_norm_g": 1.0 + 0.1 * n(ks[7], (L, CONV_WIDTH), f32),
        "w_out": n(ks[8], (L, MIX_WIDTH, D_MODEL), f32) * (MIX_WIDTH ** -0.5) * DEEPNORM_BETA,
        "ln1_g": 1.0 + 0.1 * n(ks[9], (L, D_MODEL), f32),
        "ln1_b": 0.02 * n(ks[10], (L, D_MODEL), f32),
        "w_gate": n(ks[11], (L, D_MODEL, D_FF), f32) * (D_MODEL ** -0.5),
        "w_up": n(ks[12], (L, D_MODEL, D_FF), f32) * (D_MODEL ** -0.5),
        "w_down": n(ks[13], (L, D_FF, D_MODEL), f32) * (D_FF ** -0.5) * DEEPNORM_BETA,
        "ln2_g": 1.0 + 0.1 * n(ks[14], (L, D_MODEL), f32),
        "ln2_b": 0.02 * n(ks[15], (L, D_MODEL), f32),
    }


def reference(x, c, w_ada, b_ada, w_in, conv_w, attn_norm_g, conv_norm_g, w_out,
              ln1_g, ln1_b, w_gate, w_up, w_down, ln2_g, ln2_b):
    s = x.shape[1]
    cos, sin = rope_tables(s)
    c_act = jax.nn.silu(c)
    for l in range(DEPTH):
        mod = c_act @ w_ada[l] + b_ada[l]
        sh1, sc1, g1, sh2, sc2, g2 = jnp.split(mod, 6, axis=-1)
        mix = token_mixer(modulate(x, sh1, sc1), w_in[l], conv_w[l], attn_norm_g[l],
                          conv_norm_g[l], w_out[l], cos, sin)
        x = layer_norm(DEEPNORM_ALPHA * x + g1[:, None, :] * mix, ln1_g[l], ln1_b[l])
        ff = swiglu(modulate(x, sh2, sc2), w_gate[l], w_up[l], w_down[l])
        x = layer_norm(DEEPNORM_ALPHA * x + g2[:, None, :] * ff, ln2_g[l], ln2_b[l])
    return x
```

```python
import functools

import numpy as np
import jax
import jax.numpy as jnp
from jax import lax
from jax.experimental import pallas as pl
from jax.experimental.pallas import tpu as pltpu

D_MODEL = 1024
N_HEADS = 8
HEAD_DIM = 64
ATTN_WIDTH = N_HEADS * HEAD_DIM
IDX_HEADS = 8
IDX_DIM = 32
TOPK_MAX = 256
CONV_WIDTH = D_MODEL - ATTN_WIDTH
CONV_GROUPS = 8
CONV_K = 3
ROPE_THETA = 500000.0
ROPE_DIM = HEAD_DIM // 4
D_FF = 2816
DEPTH = 1
DEEPNORM_ALPHA = (2.0 * DEPTH) ** 0.25
EPS = 1e-5
NEG_INF = -1e30

F32 = jnp.float32
BF16 = jnp.bfloat16
I32 = jnp.int32

FRONT = 1024
OFF_Q, OFF_K, OFF_V, OFF_QI, OFF_KI, OFF_WI = 0, 512, 576, 640, 896, 928
N_COLS = 2 * FRONT + 3 * CONV_WIDTH

TM_PROJ = 512
TQ = 128
KC = 512
LANES = 128
VMEM_LIMIT = 56 * 1024 * 1024


def _sortable_key(x):
    bits = lax.bitcast_convert_type(x, I32)
    return bits ^ (lax.shift_right_arithmetic(bits, 31) & jnp.int32(0x7FFFFFFF))


_NEG_BITS = int(np.array(NEG_INF, np.float32).view(np.int32))
NEG_KEY = _NEG_BITS ^ ((_NEG_BITS >> 31) & 0x7FFFFFFF)


def _dot(a, b):
    return jnp.dot(a, b, preferred_element_type=F32)


def _dot_nt(a, b):
    return lax.dot_general(a, b, (((1,), (1,)), ((), ())), preferred_element_type=F32)


def _split_bf16(a):
    hi = a.astype(BF16)
    lo = (a - hi.astype(F32)).astype(BF16)
    return hi, lo


def _ada_kernel(c_ref, w_ref, b_ref, o_ref):
    c = c_ref[...]
    ca = c * (1.0 / (1.0 + jnp.exp(-c)))
    ca_hi, ca_lo = _split_bf16(ca)
    w_hi, w_lo = _split_bf16(w_ref[...])
    acc = _dot(ca_hi, w_hi) + _dot(ca_hi, w_lo) + _dot(ca_lo, w_hi)
    o_ref[...] = acc + b_ref[...]


def _ada(c, w_ada, b_ada):
    bsz, d = c.shape
    n = w_ada.shape[1]
    blk = 1024
    return pl.pallas_call(
        _ada_kernel,
        grid=(n // blk,),
        in_specs=[
            pl.BlockSpec((bsz, d), lambda j: (0, 0)),
            pl.BlockSpec((d, blk), lambda j: (0, j)),
            pl.BlockSpec((1, blk), lambda j: (0, j)),
        ],
        out_specs=pl.BlockSpec((bsz, blk), lambda j: (0, j)),
        out_shape=jax.ShapeDtypeStruct((bsz, n), F32),
        name="ada",
    )(c, w_ada, b_ada.reshape(1, n))


def _inproj_kernel(x_ref, sh_ref, sc_ref, w_ref, ct_ref, st_ref, cw_ref, cg_ref, gm_ref,
                   q_ref, kv_ref, qi_ref, kw32_ref, kw16_ref, cv_ref, ubuf, *, tm):
    i = pl.program_id(1)
    h = (x_ref[0] * (1.0 + sc_ref[0]) + sh_ref[0]).astype(BF16)

    front = _dot(h, w_ref[:, 0:FRONT])
    partner = _dot(h, w_ref[:, FRONT:2 * FRONT])
    roped = front * ct_ref[...] + partner * st_ref[...]
    q_ref[0] = roped[:, OFF_Q:OFF_K].astype(BF16)
    kv_ref[0] = roped[:, OFF_K:OFF_QI].astype(BF16)
    qi_ref[0] = roped[:, OFF_QI:OFF_KI].astype(BF16)
    kw = roped[:, OFF_KI:FRONT]
    kw32_ref[0] = kw
    kw16_ref[0] = kw.astype(BF16)

    cp = _dot(h, w_ref[:, 2 * FRONT:N_COLS])
    gate_b = cp[:, 0:CONV_WIDTH]
    u = cp[:, CONV_WIDTH:2 * CONV_WIDTH] * cp[:, 2 * CONV_WIDTH:3 * CONV_WIDTH]

    @pl.when(i == 0)
    def _():
        ubuf[0:8, :] = jnp.zeros((8, CONV_WIDTH), F32)

    ubuf[8:8 + tm, :] = u
    conv = (cw_ref[0:1, :] * ubuf[6:6 + tm, :] + cw_ref[1:2, :] * ubuf[7:7 + tm, :]
            + cw_ref[2:3, :] * u)
    y = gate_b * conv
    y2_hi, y2_lo = _split_bf16(y * y)
    ssq = _dot(y2_hi, gm_ref[...]) + _dot(y2_lo, gm_ref[...])
    yn = y * lax.rsqrt(ssq * (1.0 / (CONV_WIDTH // CONV_GROUPS)) + EPS) * cg_ref[...]
    cv_ref[0] = yn.astype(BF16)
    ubuf[0:8, :] = ubuf[tm:tm + 8, :]


def _inproj(x, sh1, sc1, w_all, ctab, stab, conv_w, conv_g, gmat):
    bsz, s, d = x.shape
    tm = min(TM_PROJ, s)
    grid = (bsz, s // tm)
    tok = lambda w: pl.BlockSpec((1, tm, w), lambda b, i: (b, i, 0))
    vec = pl.BlockSpec((1, 1, d), lambda b, i: (b, 0, 0))
    full = lambda a: pl.BlockSpec(a.shape, lambda b, i: (0,) * a.ndim)
    return pl.pallas_call(
        functools.partial(_inproj_kernel, tm=tm),
        grid=grid,
        in_specs=[
            tok(d), vec, vec, full(w_all),
            pl.BlockSpec((tm, FRONT), lambda b, i: (i, 0)),
            pl.BlockSpec((tm, FRONT), lambda b, i: (i, 0)),
            full(conv_w), full(conv_g), full(gmat),
        ],
        out_specs=[tok(ATTN_WIDTH), tok(2 * HEAD_DIM), tok(IDX_HEADS * IDX_DIM),
                   tok(LANES), tok(LANES), tok(CONV_WIDTH)],
        out_shape=[
            jax.ShapeDtypeStruct((bsz, s, ATTN_WIDTH), BF16),
            jax.ShapeDtypeStruct((bsz, s, 2 * HEAD_DIM), BF16),
            jax.ShapeDtypeStruct((bsz, s, IDX_HEADS * IDX_DIM), BF16),
            jax.ShapeDtypeStruct((bsz, s, LANES), F32),
            jax.ShapeDtypeStruct((bsz, s, LANES), BF16),
            jax.ShapeDtypeStruct((bsz, s, CONV_WIDTH), BF16),
        ],
        scratch_shapes=[pltpu.VMEM((tm + 8, CONV_WIDTH), F32)],
        compiler_params=pltpu.CompilerParams(
            dimension_semantics=("arbitrary", "arbitrary"), vmem_limit_bytes=VMEM_LIMIT),
        name="inproj",
    )(x, sh1, sc1, w_all, ctab, stab, conv_w, conv_g, gmat)


def _attn_kernel(q_ref, qi_ref, wq_ref, kv_ref, kw16_ref, g_ref, o_ref, key_scr, bias_scr,
                 *, s, k_sel, tq, kc):
    j = pl.program_id(1)
    t0 = j * tq
    n_chunks = (t0 + tq + kc - 1) // kc
    n_skip = (s - n_chunks * kc).astype(F32)
    kf = float(k_sel)

    row_t = t0 + lax.broadcasted_iota(I32, (tq, kc), 0)
    lane_i = lax.broadcasted_iota(I32, (tq, kc), 1)

    qi = qi_ref[0]
    wi = wq_ref[0][:, OFF_WI - OFF_KI:OFF_WI - OFF_KI + IDX_HEADS]
    qi_h = [qi[:, hh * IDX_DIM:(hh + 1) * IDX_DIM] for hh in range(IDX_HEADS)]
    wi_h = [wi[:, hh:hh + 1] for hh in range(IDX_HEADS)]

    def score_body(c, carry):
        off = pl.multiple_of(c * kc, kc)
        ki_c = kw16_ref[0, pl.ds(off, kc), 0:IDX_DIM]
        acc = jnp.zeros((tq, kc), F32)
        for hh in range(IDX_HEADS):
            acc = acc + jnp.maximum(_dot_nt(qi_h[hh], ki_c), 0.0) * wi_h[hh]
        acc = jnp.where(off + lane_i <= row_t, acc, NEG_INF)
        key_scr[:, pl.ds(off, kc)] = _sortable_key(acc)
        return carry

    lax.fori_loop(0, n_chunks, score_body, 0)

    def lane_fold(m):
        part = m[:, 0:LANES]
        for jj in range(1, kc // LANES):
            part = part + m[:, jj * LANES:(jj + 1) * LANES]
        return part

    def count_rows(indicator):
        def body(c, cnt):
            off = pl.multiple_of(c * kc, kc)
            kk = key_scr[:, pl.ds(off, kc)]
            return cnt + lane_fold(indicator(kk, off))
        cnt = lax.fori_loop(0, n_chunks, body, jnp.zeros((tq, LANES), F32))
        return jnp.sum(cnt, axis=-1, keepdims=True)

    def count_ge(cand):
        return (count_rows(lambda kk, off: jnp.where(kk >= cand, 1.0, 0.0))
                + jnp.where(cand <= NEG_KEY, n_skip, 0.0))

    def bit_body(it, carry):
        prefix, cge = carry
        cand = prefix + lax.shift_left(jnp.int32(1), 31 - it)
        cnt = count_ge(cand)
        ok = cnt >= kf
        return jnp.where(ok, cand, prefix), jnp.where(ok, cnt, cge)

    int_min = jnp.int32(-2 ** 31)
    thr, cge = lax.fori_loop(
        0, 32, bit_body,
        (jnp.full((tq, 1), int_min, I32), jnp.full((tq, 1), float(s), F32)))

    has_ties = jnp.max(cge) > kf

    @pl.when(jnp.logical_not(has_ties))
    def _():
        def body(c, carry):
            off = pl.multiple_of(c * kc, kc)
            kk = key_scr[:, pl.ds(off, kc)]
            causal_bias = jnp.where(off + lane_i <= row_t, 0.0, NEG_INF)
            bias_scr[:, pl.ds(off, kc)] = jnp.where(kk >= thr, causal_bias, NEG_INF)
            return carry
        lax.fori_loop(0, n_chunks, body, 0)

    @pl.when(has_ties)
    def _():
        need = kf - count_ge(thr + 1)

        n_bits = int(s).bit_length()

        def tie_body(it, jm):
            cand = jm + lax.shift_left(jnp.int32(1), (n_bits - 1) - it)
            g = count_rows(lambda kk, off: jnp.where(
                kk == thr, jnp.where(off + lane_i < cand, 1.0, 0.0), 0.0))
            return jnp.where(g < need, cand, jm)

        jm = lax.fori_loop(0, n_bits, tie_body, jnp.zeros((tq, 1), I32))

        def body(c, carry):
            off = pl.multiple_of(c * kc, kc)
            kk = key_scr[:, pl.ds(off, kc)]
            idx = off + lane_i
            causal_bias = jnp.where(idx <= row_t, 0.0, NEG_INF)
            tie_bias = jnp.where(idx <= jm, causal_bias, NEG_INF)
            bias_scr[:, pl.ds(off, kc)] = jnp.where(
                kk > thr, causal_bias, jnp.where(kk == thr, tie_bias, NEG_INF))
            return carry
        lax.fori_loop(0, n_chunks, body, 0)

    q = q_ref[0]
    outs = []
    for hh in range(N_HEADS):
        q_h = q[:, hh * HEAD_DIM:(hh + 1) * HEAD_DIM]

        def att_body(c, carry, q_h=q_h):
            m, l, acc = carry
            off = pl.multiple_of(c * kc, kc)
            k_c = kv_ref[0, pl.ds(off, kc), 0:HEAD_DIM]
            v_c = kv_ref[0, pl.ds(off, kc), HEAD_DIM:2 * HEAD_DIM]
            logits = _dot_nt(q_h, k_c) + bias_scr[:, pl.ds(off, kc)]
            m_new = jnp.maximum(m, jnp.max(logits, axis=-1, keepdims=True))
            alpha = jnp.exp(m - m_new)
            p = jnp.exp(logits - m_new)
            l = l * alpha + jnp.sum(p, axis=-1, keepdims=True)
            acc = acc * alpha + _dot(p.astype(BF16), v_c)
            return m_new, l, acc

        m0 = jnp.full((tq, 1), -3e38, F32)
        _, l, acc = lax.fori_loop(
            0, n_chunks, att_body,
            (m0, jnp.zeros((tq, 1), F32), jnp.zeros((tq, HEAD_DIM), F32)))
        o = acc / l
        ms = jnp.mean(o * o, axis=-1, keepdims=True)
        outs.append(o * lax.rsqrt(ms + EPS) * g_ref[:, hh * HEAD_DIM:(hh + 1) * HEAD_DIM])
    o_ref[0] = jnp.concatenate(outs, axis=-1).astype(BF16)


def _attention(q, qi, kw32, kv, kw16, attn_g):
    bsz, s, _ = q.shape
    tq = min(TQ, s)
    kc = min(KC, s)
    k_sel = min(TOPK_MAX, s // 4)
    grid = (bsz, s // tq)
    qtile = lambda w: pl.BlockSpec((1, tq, w), lambda b, j: (b, j, 0))
    seq = lambda w: pl.BlockSpec((1, s, w), lambda b, j: (b, 0, 0))
    return pl.pallas_call(
        functools.partial(_attn_kernel, s=s, k_sel=k_sel, tq=tq, kc=kc),
        grid=grid,
        in_specs=[qtile(ATTN_WIDTH), qtile(IDX_HEADS * IDX_DIM), qtile(LANES),
                  seq(2 * HEAD_DIM), seq(LANES),
                  pl.BlockSpec((1, ATTN_WIDTH), lambda b, j: (0, 0))],
        out_specs=qtile(ATTN_WIDTH),
        out_shape=jax.ShapeDtypeStruct((bsz, s, ATTN_WIDTH), BF16),
        scratch_shapes=[pltpu.VMEM((tq, s), I32), pltpu.VMEM((tq, s), F32)],
        compiler_params=pltpu.CompilerParams(
            dimension_semantics=("arbitrary", "arbitrary"), vmem_limit_bytes=VMEM_LIMIT),
        name="attn",
    )(q, qi, kw32, kv, kw16, attn_g)


def _layer_norm(y, g, b):
    mu = jnp.mean(y, axis=-1, keepdims=True)
    yc = y - mu
    var = jnp.mean(yc * yc, axis=-1, keepdims=True)
    return yc * lax.rsqrt(var + EPS) * g + b


def _post_kernel(x_ref, at_ref, cv_ref, g1_ref, sh2_ref, sc2_ref, g2_ref,
                 wo_ref, l1g_ref, l1b_ref, wg_ref, wu_ref, wd_ref, l2g_ref, l2b_ref, o_ref):
    mix = _dot(at_ref[0], wo_ref[0:ATTN_WIDTH, :]) + _dot(cv_ref[0], wo_ref[ATTN_WIDTH:D_MODEL, :])
    x1 = _layer_norm(DEEPNORM_ALPHA * x_ref[0] + g1_ref[0] * mix, l1g_ref[...], l1b_ref[...])
    h2 = (x1 * (1.0 + sc2_ref[0]) + sh2_ref[0]).astype(BF16)
    gt = _dot(h2, wg_ref[...])
    up = _dot(h2, wu_ref[...])
    hid = (gt * (1.0 / (1.0 + jnp.exp(-gt))) * up).astype(BF16)
    ff = _dot(hid, wd_ref[...])
    o_ref[0] = _layer_norm(DEEPNORM_ALPHA * x1 + g2_ref[0] * ff, l2g_ref[...], l2b_ref[...])


def _post(x, attn, conv, g1, sh2, sc2, g2, w_out, l1g, l1b, w_gate, w_up, w_down, l2g, l2b):
    bsz, s, d = x.shape
    tm = min(TM_PROJ, s)
    grid = (bsz, s // tm)
    tok = lambda w: pl.BlockSpec((1, tm, w), lambda b, i: (b, i, 0))
    vec = pl.BlockSpec((1, 1, d), lambda b, i: (b, 0, 0))
    full = lambda a: pl.BlockSpec(a.shape, lambda b, i: (0,) * a.ndim,
                                  pipeline_mode=pl.Buffered(1))
    return pl.pallas_call(
        _post_kernel,
        grid=grid,
        in_specs=[tok(d), tok(ATTN_WIDTH), tok(CONV_WIDTH), vec, vec, vec, vec,
                  full(w_out), full(l1g), full(l1b), full(w_gate), full(w_up), full(w_down),
                  full(l2g), full(l2b)],
        out_specs=tok(d),
        out_shape=jax.ShapeDtypeStruct((bsz, s, d), F32),
        compiler_params=pltpu.CompilerParams(
            dimension_semantics=("arbitrary", "arbitrary"), vmem_limit_bytes=VMEM_LIMIT),
        name="post",
    )(x, attn, conv, g1, sh2, sc2, g2, w_out, l1g, l1b, w_gate, w_up, w_down, l2g, l2b)


def _front_layout():
    src = np.zeros(FRONT, np.int64)
    used = np.zeros(FRONT, bool)
    partner = np.zeros(FRONT, np.int64)
    freq = np.full(FRONT, ROPE_DIM // 2, np.int64)
    sign = np.zeros(FRONT, np.float32)
    scale = np.ones(FRONT, np.float32)
    half = ROPE_DIM // 2

    def place(dst, col, n_heads, dim, rope, sc):
        for hh in range(n_heads):
            for dd in range(dim):
                lane = dst + hh * dim + dd
                src[lane] = col + hh * dim + dd
                used[lane] = True
                scale[lane] = sc
                if rope and dd < ROPE_DIM:
                    freq[lane] = dd % half
                    if dd < half:
                        partner[lane], sign[lane] = src[lane] + half, -1.0
                    else:
                        partner[lane], sign[lane] = src[lane] - half, 1.0

    c_q, c_k, c_v, c_qi, c_ki, c_wi = 0, 512, 576, 640, 896, 928
    place(OFF_Q, c_q, N_HEADS, HEAD_DIM, True, HEAD_DIM ** -0.5)
    place(OFF_K, c_k, 1, HEAD_DIM, True, 1.0)
    place(OFF_V, c_v, 1, HEAD_DIM, False, 1.0)
    place(OFF_QI, c_qi, IDX_HEADS, IDX_DIM, True, 1.0)
    place(OFF_KI, c_ki, 1, IDX_DIM, True, 1.0)
    place(OFF_WI, c_wi, 1, IDX_HEADS, False, IDX_HEADS ** -0.5 * IDX_DIM ** -0.5)
    return src, used, partner, freq, sign, scale


def _pack_weights(w_in):
    src, used, partner, _, sign, _ = _front_layout()
    front = jnp.where(used[None, :], w_in[:, src], 0.0)
    part = jnp.where((sign != 0)[None, :], w_in[:, partner], 0.0)
    conv_cols = w_in[:, OFF_WI + IDX_HEADS:]
    return jnp.concatenate([front, part, conv_cols], axis=1).astype(BF16)


def _rope_tables(s):
    _, _, _, freq, sign, scale = _front_layout()
    pos = jnp.arange(s, dtype=F32)
    inv = ROPE_THETA ** (-(jnp.arange(0, ROPE_DIM, 2, dtype=F32) / ROPE_DIM))
    ang = pos[:, None] * inv[None, :]
    cos_ext = jnp.concatenate([jnp.cos(ang), jnp.ones((s, 1), F32)], axis=1)
    sin_ext = jnp.concatenate([jnp.sin(ang), jnp.zeros((s, 1), F32)], axis=1)
    ctab = cos_ext[:, freq] * scale[None, :]
    stab = sin_ext[:, freq] * (sign * scale)[None, :]
    return ctab, stab


def kernel(x, c, w_ada, b_ada, w_in, conv_w, attn_norm_g, conv_norm_g, w_out, ln1_g, ln1_b,
           w_gate, w_up, w_down, ln2_g, ln2_b):
    bsz, s, d = x.shape
    ctab, stab = _rope_tables(s)
    grp = CONV_WIDTH // CONV_GROUPS
    gmat = jnp.asarray(np.kron(np.eye(CONV_GROUPS), np.ones((grp, grp))), BF16)
    for l in range(DEPTH):
        mod = _ada(c, w_ada[l], b_ada[l])
        sh1, sc1, g1, sh2, sc2, g2 = [m.reshape(bsz, 1, d) for m in jnp.split(mod, 6, axis=-1)]
        q, kv, qi, kw32, kw16, conv = _inproj(
            x, sh1, sc1, _pack_weights(w_in[l]), ctab, stab, conv_w[l],
            conv_norm_g[l].reshape(1, -1), gmat)
        attn = _attention(q, qi, kw32, kv, kw16, attn_norm_g[l].reshape(1, -1))
        x = _post(x, attn, conv, g1, sh2, sc2, g2,
                  w_out[l].astype(BF16), ln1_g[l].reshape(1, -1), ln1_b[l].reshape(1, -1),
                  w_gate[l].astype(BF16), w_up[l].astype(BF16), w_down[l].astype(BF16),
                  ln2_g[l].reshape(1, -1), ln2_b[l].reshape(1, -1))
    return x
```

```python
import functools

import numpy as np
import jax
import jax.numpy as jnp
from jax import lax
from jax.experimental import pallas as pl
from jax.experimental.pallas import tpu as pltpu

D_MODEL = 1024
N_HEADS = 8
HEAD_DIM = 64
ATTN_WIDTH = N_HEADS * HEAD_DIM
IDX_HEADS = 8
IDX_DIM = 32
TOPK_MAX = 256
CONV_WIDTH = D_MODEL - ATTN_WIDTH
CONV_GROUPS = 8
CONV_K = 3
ROPE_THETA = 500000.0
ROPE_DIM = HEAD_DIM // 4
D_FF = 2816
DEPTH = 1
DEEPNORM_ALPHA = (2.0 * DEPTH) ** 0.25
EPS = 1e-5
NEG_INF = -1e30

F32 = jnp.float32
BF16 = jnp.bfloat16
I32 = jnp.int32

FRONT = 1024
OFF_Q, OFF_K, OFF_V, OFF_QI, OFF_KI, OFF_WI = 0, 512, 576, 640, 896, 928
N_COLS = 2 * FRONT + 3 * CONV_WIDTH

TM_PROJ = 512
TQ = 256
KC = 512
CNT_ROWS = 64
LANES = 128
VMEM_LIMIT = 56 * 1024 * 1024


def _sortable_key(x):
    bits = lax.bitcast_convert_type(x, I32)
    return bits ^ (lax.shift_right_arithmetic(bits, 31) & jnp.int32(0x7FFFFFFF))


_NEG_BITS = int(np.array(NEG_INF, np.float32).view(np.int32))
NEG_KEY = _NEG_BITS ^ ((_NEG_BITS >> 31) & 0x7FFFFFFF)


def _dot(a, b):
    return jnp.dot(a, b, preferred_element_type=F32)


def _dot_nt(a, b):
    return lax.dot_general(a, b, (((1,), (1,)), ((), ())), preferred_element_type=F32)


def _split_bf16(a):
    hi = a.astype(BF16)
    lo = (a - hi.astype(F32)).astype(BF16)
    return hi, lo


def _ada_kernel(c_ref, w_ref, b_ref, o_ref):
    c = c_ref[...]
    ca = c * (1.0 / (1.0 + jnp.exp(-c)))
    ca_hi, ca_lo = _split_bf16(ca)
    w_hi, w_lo = _split_bf16(w_ref[...])
    acc = _dot(ca_hi, w_hi) + _dot(ca_hi, w_lo) + _dot(ca_lo, w_hi)
    o_ref[...] = acc + b_ref[...]


def _ada(c, w_ada, b_ada):
    bsz, d = c.shape
    n = w_ada.shape[1]
    blk = 1024
    return pl.pallas_call(
        _ada_kernel,
        grid=(n // blk,),
        in_specs=[
            pl.BlockSpec((bsz, d), lambda j: (0, 0)),
            pl.BlockSpec((d, blk), lambda j: (0, j)),
            pl.BlockSpec((1, blk), lambda j: (0, j)),
        ],
        out_specs=pl.BlockSpec((bsz, blk), lambda j: (0, j)),
        out_shape=jax.ShapeDtypeStruct((bsz, n), F32),
        name="ada",
    )(c, w_ada, b_ada.reshape(1, n))


def _inproj_kernel(x_ref, sh_ref, sc_ref, w_ref, ct_ref, st_ref, cw_ref, cg_ref, gm_ref,
                   q_ref, kv_ref, qi_ref, ki_ref, vt_ref, wt_ref, cv_ref, ubuf, *, tm):
    i = pl.program_id(1)
    h = (x_ref[0] * (1.0 + sc_ref[0]) + sh_ref[0]).astype(BF16)

    front = _dot(h, w_ref[:, 0:FRONT])
    partner = _dot(h, w_ref[:, FRONT:2 * FRONT])
    roped = front * ct_ref[...] + partner * st_ref[...]
    q_ref[0] = roped[:, OFF_Q:OFF_K].astype(BF16)
    kv = roped[:, OFF_K:OFF_QI]
    kv_ref[0] = kv.astype(BF16)
    qi_ref[0] = roped[:, OFF_QI:OFF_KI].astype(BF16)
    kw = roped[:, OFF_KI:FRONT]
    ki_ref[0] = kw.astype(BF16)
    vt_ref[0] = jnp.transpose(kv)[HEAD_DIM:2 * HEAD_DIM, :].astype(BF16)
    wt_ref[0] = jnp.transpose(kw)[OFF_WI - OFF_KI:OFF_WI - OFF_KI + IDX_HEADS, :]

    cp = _dot(h, w_ref[:, 2 * FRONT:N_COLS])
    gate_b = cp[:, 0:CONV_WIDTH]
    u = cp[:, CONV_WIDTH:2 * CONV_WIDTH] * cp[:, 2 * CONV_WIDTH:3 * CONV_WIDTH]

    @pl.when(i == 0)
    def _():
        ubuf[0:8, :] = jnp.zeros((8, CONV_WIDTH), F32)

    ubuf[8:8 + tm, :] = u
    conv = (cw_ref[0:1, :] * ubuf[6:6 + tm, :] + cw_ref[1:2, :] * ubuf[7:7 + tm, :]
            + cw_ref[2:3, :] * u)
    y = gate_b * conv
    y2_hi, y2_lo = _split_bf16(y * y)
    ssq = _dot(y2_hi, gm_ref[...]) + _dot(y2_lo, gm_ref[...])
    yn = y * lax.rsqrt(ssq * (1.0 / (CONV_WIDTH // CONV_GROUPS)) + EPS) * cg_ref[...]
    cv_ref[0] = yn.astype(BF16)
    ubuf[0:8, :] = ubuf[tm:tm + 8, :]


def _inproj(x, sh1, sc1, w_all, ctab, stab, conv_w, conv_g, gmat):
    bsz, s, d = x.shape
    tm = min(TM_PROJ, s)
    grid = (bsz, s // tm)
    tok = lambda w: pl.BlockSpec((1, tm, w), lambda b, i: (b, i, 0))
    tok_t = lambda r: pl.BlockSpec((1, r, tm), lambda b, i: (b, 0, i))
    vec = pl.BlockSpec((1, 1, d), lambda b, i: (b, 0, 0))
    full = lambda a: pl.BlockSpec(a.shape, lambda b, i: (0,) * a.ndim)
    return pl.pallas_call(
        functools.partial(_inproj_kernel, tm=tm),
        grid=grid,
        in_specs=[
            tok(d), vec, vec, full(w_all),
            pl.BlockSpec((tm, FRONT), lambda b, i: (i, 0)),
            pl.BlockSpec((tm, FRONT), lambda b, i: (i, 0)),
            full(conv_w), full(conv_g), full(gmat),
        ],
        out_specs=[tok(ATTN_WIDTH), tok(2 * HEAD_DIM), tok(IDX_HEADS * IDX_DIM),
                   tok(LANES), tok_t(HEAD_DIM), tok_t(IDX_HEADS), tok(CONV_WIDTH)],
        out_shape=[
            jax.ShapeDtypeStruct((bsz, s, ATTN_WIDTH), BF16),
            jax.ShapeDtypeStruct((bsz, s, 2 * HEAD_DIM), BF16),
            jax.ShapeDtypeStruct((bsz, s, IDX_HEADS * IDX_DIM), BF16),
            jax.ShapeDtypeStruct((bsz, s, LANES), BF16),
            jax.ShapeDtypeStruct((bsz, HEAD_DIM, s), BF16),
            jax.ShapeDtypeStruct((bsz, IDX_HEADS, s), F32),
            jax.ShapeDtypeStruct((bsz, s, CONV_WIDTH), BF16),
        ],
        scratch_shapes=[pltpu.VMEM((tm + 8, CONV_WIDTH), F32)],
        compiler_params=pltpu.CompilerParams(
            dimension_semantics=("arbitrary", "arbitrary"), vmem_limit_bytes=VMEM_LIMIT),
        name="inproj",
    )(x, sh1, sc1, w_all, ctab, stab, conv_w, conv_g, gmat)


def _attn_kernel(q_ref, qi_ref, wt_ref, kv_ref, ki_ref, vt_ref, g_ref, o_ref,
                 key_scr, bias_scr, p_scr, acc_scr, *, s, k_sel, tq, kc):
    j = pl.program_id(1)
    t0 = j * tq
    n_chunks = (t0 + tq + kc - 1) // kc
    n_skip = (s - n_chunks * kc).astype(F32)
    kf = float(k_sel)

    key_i = lax.broadcasted_iota(I32, (kc, tq), 0)
    qry_t = t0 + lax.broadcasted_iota(I32, (kc, tq), 1)

    def fold_rows(a, rows):
        return jnp.sum(a.reshape(kc // rows, rows, tq), axis=0)

    qi = qi_ref[0]
    qi_h = [qi[:, hh * IDX_DIM:(hh + 1) * IDX_DIM] for hh in range(IDX_HEADS)]
    wt = wt_ref[0]
    wt_h = [wt[hh:hh + 1, :] for hh in range(IDX_HEADS)]

    def score_body(c, carry):
        off = pl.multiple_of(c * kc, kc)
        ki_c = ki_ref[0, pl.ds(off, kc), 0:IDX_DIM]
        acc = jnp.zeros((kc, tq), F32)
        for hh in range(IDX_HEADS):
            acc = acc + jnp.maximum(_dot_nt(ki_c, qi_h[hh]), 0.0) * wt_h[hh]
        acc = jnp.where(off + key_i <= qry_t, acc, NEG_INF)
        key_scr[pl.ds(off, kc), :] = _sortable_key(acc)
        return carry

    lax.fori_loop(0, n_chunks, score_body, 0)

    def count_keys(indicator):
        def body(c, cnt):
            off = pl.multiple_of(c * kc, kc)
            return cnt + fold_rows(indicator(key_scr[pl.ds(off, kc), :], off), CNT_ROWS)
        cnt = lax.fori_loop(0, n_chunks, body, jnp.zeros((CNT_ROWS, tq), F32))
        return jnp.sum(cnt, axis=0, keepdims=True)

    def count_ge(cand):
        return (count_keys(lambda kk, off: jnp.where(kk >= cand, 1.0, 0.0))
                + jnp.where(cand <= NEG_KEY, n_skip, 0.0))

    def bit_body(it, carry):
        prefix, cge = carry
        cand = prefix + lax.shift_left(jnp.int32(1), 31 - it)
        cnt = count_ge(cand)
        ok = cnt >= kf
        return jnp.where(ok, cand, prefix), jnp.where(ok, cnt, cge)

    int_min = jnp.int32(-2 ** 31)
    thr, cge = lax.fori_loop(
        0, 32, bit_body,
        (jnp.full((1, tq), int_min, I32), jnp.full((1, tq), float(s), F32)))

    has_ties = jnp.max(cge) > kf

    @pl.when(jnp.logical_not(has_ties))
    def _():
        def body(c, carry):
            off = pl.multiple_of(c * kc, kc)
            kk = key_scr[pl.ds(off, kc), :]
            causal_bias = jnp.where(off + key_i <= qry_t, 0.0, NEG_INF)
            bias_scr[pl.ds(off, kc), :] = jnp.where(kk >= thr, causal_bias, NEG_INF)
            return carry
        lax.fori_loop(0, n_chunks, body, 0)

    @pl.when(has_ties)
    def _():
        need = kf - count_ge(thr + 1)
        n_bits = int(s).bit_length()

        def tie_body(it, jm):
            cand = jm + lax.shift_left(jnp.int32(1), (n_bits - 1) - it)
            g = count_keys(lambda kk, off: jnp.where(
                kk == thr, jnp.where(off + key_i < cand, 1.0, 0.0), 0.0))
            return jnp.where(g < need, cand, jm)

        jm = lax.fori_loop(0, n_bits, tie_body, jnp.zeros((1, tq), I32))

        def body(c, carry):
            off = pl.multiple_of(c * kc, kc)
            kk = key_scr[pl.ds(off, kc), :]
            idx = off + key_i
            causal_bias = jnp.where(idx <= qry_t, 0.0, NEG_INF)
            tie_bias = jnp.where(idx <= jm, causal_bias, NEG_INF)
            bias_scr[pl.ds(off, kc), :] = jnp.where(
                kk > thr, causal_bias, jnp.where(kk == thr, tie_bias, NEG_INF))
            return carry
        lax.fori_loop(0, n_chunks, body, 0)

    q = q_ref[0]
    q_h = [q[:, hh * HEAD_DIM:(hh + 1) * HEAD_DIM] for hh in range(N_HEADS)]

    def masked_logits(c, hh):
        off = pl.multiple_of(c * kc, kc)
        k_c = kv_ref[0, pl.ds(off, kc), 0:HEAD_DIM]
        return _dot_nt(k_c, q_h[hh]) + bias_scr[pl.ds(off, kc), :]

    def max_body(c, mparts):
        return tuple(
            jnp.maximum(mparts[hh], jnp.max(masked_logits(c, hh).reshape(kc // 8, 8, tq), axis=0))
            for hh in range(N_HEADS))

    mparts = lax.fori_loop(0, n_chunks, max_body,
                           tuple(jnp.full((8, tq), -3e38, F32) for _ in range(N_HEADS)))
    m_h = [jnp.max(mp, axis=0, keepdims=True) for mp in mparts]

    def exp_body(c, lparts):
        off = pl.multiple_of(c * kc, kc)
        new = []
        for hh in range(N_HEADS):
            p = jnp.exp(masked_logits(c, hh) - m_h[hh])
            new.append(lparts[hh] + fold_rows(p, 8))
            p_scr[pl.ds(off, kc), hh * tq:(hh + 1) * tq] = p.astype(BF16)
        return tuple(new)

    lparts = lax.fori_loop(0, n_chunks, exp_body,
                           tuple(jnp.zeros((8, tq), F32) for _ in range(N_HEADS)))

    acc_scr[...] = jnp.zeros((HEAD_DIM, N_HEADS * tq), F32)

    def pv_body(c, carry):
        off = pl.multiple_of(c * kc, kc)
        acc_scr[...] += _dot(vt_ref[0, :, pl.ds(off, kc)], p_scr[pl.ds(off, kc), :])
        return carry

    lax.fori_loop(0, n_chunks, pv_body, 0)

    outs = []
    for hh in range(N_HEADS):
        o = (acc_scr[:, hh * tq:(hh + 1) * tq]
             / jnp.sum(lparts[hh], axis=0, keepdims=True))
        ms = jnp.mean(o * o, axis=0, keepdims=True)
        outs.append(o * lax.rsqrt(ms + EPS))
    out_t = jnp.concatenate(outs, axis=0)
    o_ref[0] = (jnp.transpose(out_t) * g_ref[...]).astype(BF16)


def _attention(q, qi, wt, kv, ki, vt, attn_g):
    bsz, s, _ = q.shape
    tq = min(TQ, s)
    kc = min(KC, s)
    k_sel = min(TOPK_MAX, s // 4)
    grid = (bsz, s // tq)
    qtile = lambda w: pl.BlockSpec((1, tq, w), lambda b, j: (b, j, 0))
    seq = lambda w: pl.BlockSpec((1, s, w), lambda b, j: (b, 0, 0))
    return pl.pallas_call(
        functools.partial(_attn_kernel, s=s, k_sel=k_sel, tq=tq, kc=kc),
        grid=grid,
        in_specs=[qtile(ATTN_WIDTH), qtile(IDX_HEADS * IDX_DIM),
                  pl.BlockSpec((1, IDX_HEADS, tq), lambda b, j: (b, 0, j)),
                  seq(2 * HEAD_DIM), seq(LANES),
                  pl.BlockSpec((1, HEAD_DIM, s), lambda b, j: (b, 0, 0)),
                  pl.BlockSpec((1, ATTN_WIDTH), lambda b, j: (0, 0))],
        out_specs=qtile(ATTN_WIDTH),
        out_shape=jax.ShapeDtypeStruct((bsz, s, ATTN_WIDTH), BF16),
        scratch_shapes=[pltpu.VMEM((s, tq), I32), pltpu.VMEM((s, tq), F32),
                        pltpu.VMEM((s, N_HEADS * tq), BF16),
                        pltpu.VMEM((HEAD_DIM, N_HEADS * tq), F32)],
        compiler_params=pltpu.CompilerParams(
            dimension_semantics=("arbitrary", "arbitrary"), vmem_limit_bytes=VMEM_LIMIT),
        name="attn",
    )(q, qi, wt, kv, ki, vt, attn_g)


def _layer_norm(y, g, b):
    mu = jnp.mean(y, axis=-1, keepdims=True)
    yc = y - mu
    var = jnp.mean(yc * yc, axis=-1, keepdims=True)
    return yc * lax.rsqrt(var + EPS) * g + b


def _post_kernel(x_ref, at_ref, cv_ref, g1_ref, sh2_ref, sc2_ref, g2_ref,
                 wo_ref, l1g_ref, l1b_ref, wg_ref, wu_ref, wd_ref, l2g_ref, l2b_ref, o_ref):
    mix = _dot(at_ref[0], wo_ref[0:ATTN_WIDTH, :]) + _dot(cv_ref[0], wo_ref[ATTN_WIDTH:D_MODEL, :])
    x1 = _layer_norm(DEEPNORM_ALPHA * x_ref[0] + g1_ref[0] * mix, l1g_ref[...], l1b_ref[...])
    h2 = (x1 * (1.0 + sc2_ref[0]) + sh2_ref[0]).astype(BF16)
    gt = _dot(h2, wg_ref[...])
    up = _dot(h2, wu_ref[...])
    hid = (gt * (1.0 / (1.0 + jnp.exp(-gt))) * up).astype(BF16)
    ff = _dot(hid, wd_ref[...])
    o_ref[0] = _layer_norm(DEEPNORM_ALPHA * x1 + g2_ref[0] * ff, l2g_ref[...], l2b_ref[...])


def _post(x, attn, conv, g1, sh2, sc2, g2, w_out, l1g, l1b, w_gate, w_up, w_down, l2g, l2b):
    bsz, s, d = x.shape
    tm = min(TM_PROJ, s)
    grid = (bsz, s // tm)
    tok = lambda w: pl.BlockSpec((1, tm, w), lambda b, i: (b, i, 0))
    vec = pl.BlockSpec((1, 1, d), lambda b, i: (b, 0, 0))
    full = lambda a: pl.BlockSpec(a.shape, lambda b, i: (0,) * a.ndim,
                                  pipeline_mode=pl.Buffered(1))
    return pl.pallas_call(
        _post_kernel,
        grid=grid,
        in_specs=[tok(d), tok(ATTN_WIDTH), tok(CONV_WIDTH), vec, vec, vec, vec,
                  full(w_out), full(l1g), full(l1b), full(w_gate), full(w_up), full(w_down),
                  full(l2g), full(l2b)],
        out_specs=tok(d),
        out_shape=jax.ShapeDtypeStruct((bsz, s, d), F32),
        compiler_params=pltpu.CompilerParams(
            dimension_semantics=("arbitrary", "arbitrary"), vmem_limit_bytes=VMEM_LIMIT),
        name="post",
    )(x, attn, conv, g1, sh2, sc2, g2, w_out, l1g, l1b, w_gate, w_up, w_down, l2g, l2b)


def _front_layout():
    src = np.zeros(FRONT, np.int64)
    used = np.zeros(FRONT, bool)
    partner = np.zeros(FRONT, np.int64)
    freq = np.full(FRONT, ROPE_DIM // 2, np.int64)
    sign = np.zeros(FRONT, np.float32)
    scale = np.ones(FRONT, np.float32)
    half = ROPE_DIM // 2

    def place(dst, col, n_heads, dim, rope, sc):
        for hh in range(n_heads):
            for dd in range(dim):
                lane = dst + hh * dim + dd
                src[lane] = col + hh * dim + dd
                used[lane] = True
                scale[lane] = sc
                if rope and dd < ROPE_DIM:
                    freq[lane] = dd % half
                    if dd < half:
                        partner[lane], sign[lane] = src[lane] + half, -1.0
                    else:
                        partner[lane], sign[lane] = src[lane] - half, 1.0

    c_q, c_k, c_v, c_qi, c_ki, c_wi = 0, 512, 576, 640, 896, 928
    place(OFF_Q, c_q, N_HEADS, HEAD_DIM, True, HEAD_DIM ** -0.5)
    place(OFF_K, c_k, 1, HEAD_DIM, True, 1.0)
    place(OFF_V, c_v, 1, HEAD_DIM, False, 1.0)
    place(OFF_QI, c_qi, IDX_HEADS, IDX_DIM, True, 1.0)
    place(OFF_KI, c_ki, 1, IDX_DIM, True, 1.0)
    place(OFF_WI, c_wi, 1, IDX_HEADS, False, IDX_HEADS ** -0.5 * IDX_DIM ** -0.5)
    return src, used, partner, freq, sign, scale


def _pack_weights(w_in):
    src, used, partner, _, sign, _ = _front_layout()
    front = jnp.where(used[None, :], w_in[:, src], 0.0)
    part = jnp.where((sign != 0)[None, :], w_in[:, partner], 0.0)
    conv_cols = w_in[:, OFF_WI + IDX_HEADS:]
    return jnp.concatenate([front, part, conv_cols], axis=1).astype(BF16)


def _rope_tables(s):
    _, _, _, freq, sign, scale = _front_layout()
    pos = jnp.arange(s, dtype=F32)
    inv = ROPE_THETA ** (-(jnp.arange(0, ROPE_DIM, 2, dtype=F32) / ROPE_DIM))
    ang = pos[:, None] * inv[None, :]
    cos_ext = jnp.concatenate([jnp.cos(ang), jnp.ones((s, 1), F32)], axis=1)
    sin_ext = jnp.concatenate([jnp.sin(ang), jnp.zeros((s, 1), F32)], axis=1)
    ctab = cos_ext[:, freq] * scale[None, :]
    stab = sin_ext[:, freq] * (sign * scale)[None, :]
    return ctab, stab


def kernel(x, c, w_ada, b_ada, w_in, conv_w, attn_norm_g, conv_norm_g, w_out, ln1_g, ln1_b,
           w_gate, w_up, w_down, ln2_g, ln2_b):
    bsz, s, d = x.shape
    ctab, stab = _rope_tables(s)
    grp = CONV_WIDTH // CONV_GROUPS
    gmat = jnp.asarray(np.kron(np.eye(CONV_GROUPS), np.ones((grp, grp))), BF16)
    for l in range(DEPTH):
        mod = _ada(c, w_ada[l], b_ada[l])
        sh1, sc1, g1, sh2, sc2, g2 = [m.reshape(bsz, 1, d) for m in jnp.split(mod, 6, axis=-1)]
        q, kv, qi, ki, vt, wt, conv = _inproj(
            x, sh1, sc1, _pack_weights(w_in[l]), ctab, stab, conv_w[l],
            conv_norm_g[l].reshape(1, -1), gmat)
        attn = _attention(q, qi, wt, kv, ki, vt, attn_norm_g[l].reshape(1, -1))
        x = _post(x, attn, conv, g1, sh2, sc2, g2,
                  w_out[l].astype(BF16), ln1_g[l].reshape(1, -1), ln1_b[l].reshape(1, -1),
                  w_gate[l].astype(BF16), w_up[l].astype(BF16), w_down[l].astype(BF16),
                  ln2_g[l].reshape(1, -1), ln2_b[l].reshape(1, -1))
    return x
```

```python
import functools

import numpy as np
import jax
import jax.numpy as jnp
from jax import lax
from jax.experimental import pallas as pl
from jax.experimental.pallas import tpu as pltpu

D_MODEL = 1024
N_HEADS = 8
HEAD_DIM = 64
ATTN_WIDTH = N_HEADS * HEAD_DIM
IDX_HEADS = 8
IDX_DIM = 32
TOPK_MAX = 256
CONV_WIDTH = D_MODEL - ATTN_WIDTH
CONV_GROUPS = 8
CONV_K = 3
ROPE_THETA = 500000.0
ROPE_DIM = HEAD_DIM // 4
D_FF = 2816
DEPTH = 1
DEEPNORM_ALPHA = (2.0 * DEPTH) ** 0.25
EPS = 1e-5
NEG_INF = -1e30

F32 = jnp.float32
BF16 = jnp.bfloat16
I32 = jnp.int32
I16 = jnp.int16

FRONT = 1024
OFF_Q, OFF_K, OFF_V, OFF_QI, OFF_KI, OFF_WI = 0, 512, 576, 640, 896, 928
N_COLS = 2 * FRONT + 3 * CONV_WIDTH

TM_PROJ = 512
TQ = 256
KC = 512
CNT_ROWS = 64
VT_ROWS = 80
LOG2E = 1.4426950408889634
BOUND_SLACK = 1.002
MIN_DENOM = 2.0 ** -60
LANES = 128
VMEM_LIMIT = 56 * 1024 * 1024


def _sortable_key(x):
    bits = lax.bitcast_convert_type(x, I32)
    return bits ^ (lax.shift_right_arithmetic(bits, 31) & jnp.int32(0x7FFFFFFF))


_NEG_BITS = int(np.array(NEG_INF, np.float32).view(np.int32))
NEG_KEY = _NEG_BITS ^ ((_NEG_BITS >> 31) & 0x7FFFFFFF)
NEG_HI = NEG_KEY >> 16
NEG_LO = (NEG_KEY & 0xFFFF) - 32768


def _dot(a, b):
    return jnp.dot(a, b, preferred_element_type=F32)


def _dot_nt(a, b):
    return lax.dot_general(a, b, (((1,), (1,)), ((), ())), preferred_element_type=F32)


def _split_bf16(a):
    hi = a.astype(BF16)
    lo = (a - hi.astype(F32)).astype(BF16)
    return hi, lo


def _ada_kernel(c_ref, w_ref, b_ref, o_ref):
    c = c_ref[...]
    ca = c * (1.0 / (1.0 + jnp.exp(-c)))
    ca_hi, ca_lo = _split_bf16(ca)
    w_hi, w_lo = _split_bf16(w_ref[...])
    acc = _dot(ca_hi, w_hi) + _dot(ca_hi, w_lo) + _dot(ca_lo, w_hi)
    o_ref[...] = acc + b_ref[...]


def _ada(c, w_ada, b_ada):
    bsz, d = c.shape
    n = w_ada.shape[1]
    blk = 1024
    return pl.pallas_call(
        _ada_kernel,
        grid=(n // blk,),
        in_specs=[
            pl.BlockSpec((bsz, d), lambda j: (0, 0)),
            pl.BlockSpec((d, blk), lambda j: (0, j)),
            pl.BlockSpec((1, blk), lambda j: (0, j)),
        ],
        out_specs=pl.BlockSpec((bsz, blk), lambda j: (0, j)),
        out_shape=jax.ShapeDtypeStruct((bsz, n), F32),
        name="ada",
    )(c, w_ada, b_ada.reshape(1, n))


def _inproj_kernel(x_ref, sh_ref, sc_ref, w_ref, ct_ref, st_ref, cw_ref, cg_ref, gm_ref,
                   q_ref, kv_ref, qi_ref, ki_ref, vt_ref, wt_ref, cv_ref, ubuf, *, tm):
    i = pl.program_id(1)
    h = (x_ref[0] * (1.0 + sc_ref[0]) + sh_ref[0]).astype(BF16)

    front = _dot(h, w_ref[:, 0:FRONT])
    partner = _dot(h, w_ref[:, FRONT:2 * FRONT])
    roped = front * ct_ref[...] + partner * st_ref[...]
    q_ref[0] = roped[:, OFF_Q:OFF_K].astype(BF16)
    kv = roped[:, OFF_K:OFF_QI]
    kv_ref[0] = kv.astype(BF16)
    qi_ref[0] = roped[:, OFF_QI:OFF_KI].astype(BF16)
    kw = roped[:, OFF_KI:FRONT]
    ki_ref[0] = kw.astype(BF16)
    vt_ref[0, 0:HEAD_DIM, :] = jnp.transpose(kv)[HEAD_DIM:2 * HEAD_DIM, :].astype(BF16)
    vt_ref[0, HEAD_DIM:VT_ROWS, :] = jnp.ones((VT_ROWS - HEAD_DIM, tm), BF16)
    wt_ref[0] = jnp.transpose(kw)[OFF_WI - OFF_KI:OFF_WI - OFF_KI + IDX_HEADS, :]

    cp = _dot(h, w_ref[:, 2 * FRONT:N_COLS])
    gate_b = cp[:, 0:CONV_WIDTH]
    u = cp[:, CONV_WIDTH:2 * CONV_WIDTH] * cp[:, 2 * CONV_WIDTH:3 * CONV_WIDTH]

    @pl.when(i == 0)
    def _():
        ubuf[0:8, :] = jnp.zeros((8, CONV_WIDTH), F32)

    ubuf[8:8 + tm, :] = u
    conv = (cw_ref[0:1, :] * ubuf[6:6 + tm, :] + cw_ref[1:2, :] * ubuf[7:7 + tm, :]
            + cw_ref[2:3, :] * u)
    y = gate_b * conv
    y2_hi, y2_lo = _split_bf16(y * y)
    ssq = _dot(y2_hi, gm_ref[...]) + _dot(y2_lo, gm_ref[...])
    yn = y * lax.rsqrt(ssq * (1.0 / (CONV_WIDTH // CONV_GROUPS)) + EPS) * cg_ref[...]
    cv_ref[0] = yn.astype(BF16)
    ubuf[0:8, :] = ubuf[tm:tm + 8, :]


def _inproj(x, sh1, sc1, w_all, ctab, stab, conv_w, conv_g, gmat):
    bsz, s, d = x.shape
    tm = min(TM_PROJ, s)
    grid = (bsz, s // tm)
    tok = lambda w: pl.BlockSpec((1, tm, w), lambda b, i: (b, i, 0))
    tok_t = lambda r: pl.BlockSpec((1, r, tm), lambda b, i: (b, 0, i))
    vec = pl.BlockSpec((1, 1, d), lambda b, i: (b, 0, 0))
    full = lambda a: pl.BlockSpec(a.shape, lambda b, i: (0,) * a.ndim)
    return pl.pallas_call(
        functools.partial(_inproj_kernel, tm=tm),
        grid=grid,
        in_specs=[
            tok(d), vec, vec, full(w_all),
            pl.BlockSpec((tm, FRONT), lambda b, i: (i, 0)),
            pl.BlockSpec((tm, FRONT), lambda b, i: (i, 0)),
            full(conv_w), full(conv_g), full(gmat),
        ],
        out_specs=[tok(ATTN_WIDTH), tok(2 * HEAD_DIM), tok(IDX_HEADS * IDX_DIM),
                   tok(LANES), tok_t(VT_ROWS), tok_t(IDX_HEADS), tok(CONV_WIDTH)],
        out_shape=[
            jax.ShapeDtypeStruct((bsz, s, ATTN_WIDTH), BF16),
            jax.ShapeDtypeStruct((bsz, s, 2 * HEAD_DIM), BF16),
            jax.ShapeDtypeStruct((bsz, s, IDX_HEADS * IDX_DIM), BF16),
            jax.ShapeDtypeStruct((bsz, s, LANES), BF16),
            jax.ShapeDtypeStruct((bsz, VT_ROWS, s), BF16),
            jax.ShapeDtypeStruct((bsz, IDX_HEADS, s), F32),
            jax.ShapeDtypeStruct((bsz, s, CONV_WIDTH), BF16),
        ],
        scratch_shapes=[pltpu.VMEM((tm + 8, CONV_WIDTH), F32)],
        compiler_params=pltpu.CompilerParams(
            dimension_semantics=("arbitrary", "arbitrary"), vmem_limit_bytes=VMEM_LIMIT),
        name="inproj",
    )(x, sh1, sc1, w_all, ctab, stab, conv_w, conv_g, gmat)


def _attn_kernel(q_ref, qi_ref, wt_ref, kv_ref, ki_ref, vt_ref, g_ref, hsum_ref, o_ref,
                 key_scr, hi_scr, lo_scr, bias_scr, p_scr, acc_scr, kmax_scr,
                 *, s, k_sel, tq, kc):
    j = pl.program_id(1)
    t0 = j * tq
    n_chunks = (t0 + tq + kc - 1) // kc
    n_skip = (s - n_chunks * kc).astype(F32)
    kf = float(k_sel)

    key_i = lax.broadcasted_iota(I32, (kc, tq), 0)
    qry_t = t0 + lax.broadcasted_iota(I32, (kc, tq), 1)

    def fold_rows(a, rows):
        return jnp.sum(a.reshape(kc // rows, rows, tq), axis=0)

    qi = qi_ref[0]
    qi_h = [qi[:, hh * IDX_DIM:(hh + 1) * IDX_DIM] for hh in range(IDX_HEADS)]
    wt = wt_ref[0]
    wt_h = [wt[hh:hh + 1, :] for hh in range(IDX_HEADS)]

    def score_body(c, carry):
        off = pl.multiple_of(c * kc, kc)
        ki_c = ki_ref[0, pl.ds(off, kc), 0:IDX_DIM]
        acc = jnp.zeros((kc, tq), F32)
        for hh in range(IDX_HEADS):
            acc = acc + jnp.maximum(_dot_nt(ki_c, qi_h[hh]), 0.0) * wt_h[hh]
        acc = jnp.where(off + key_i <= qry_t, acc, NEG_INF)
        key = _sortable_key(acc)
        key_scr[pl.ds(off, kc), :] = key
        hi_scr[pl.ds(off, kc), :] = lax.shift_right_arithmetic(key, 16).astype(I16)
        lo_scr[pl.ds(off, kc), :] = ((key & 0xFFFF) - 32768).astype(I16)
        return carry

    lax.fori_loop(0, n_chunks, score_body, 0)

    def count16(scr, cand):
        cand16 = cand.astype(I16)

        def body(c, cnt):
            off = pl.multiple_of(c * kc, kc)
            ind = jnp.where(scr[pl.ds(off, kc), :] >= cand16, jnp.int16(1), jnp.int16(0))
            parts = [ind[r * CNT_ROWS:(r + 1) * CNT_ROWS, :] for r in range(kc // CNT_ROWS)]
            while len(parts) > 1:
                parts = [parts[i] + parts[i + 1] for i in range(0, len(parts), 2)]
            return cnt + parts[0]

        cnt = lax.fori_loop(0, n_chunks, body, jnp.zeros((CNT_ROWS, tq), I16))
        return jnp.sum(cnt.astype(F32), axis=0, keepdims=True)

    def radix16(scr, k_need, cge0, skipped_ge):
        def bit_body(it, carry):
            prefix, cge, cgt = carry
            cand = prefix + lax.shift_left(jnp.int32(1), 15 - it)
            cnt = count16(scr, cand) + skipped_ge(cand)
            ok = cnt >= k_need
            return jnp.where(ok, cand, prefix), jnp.where(ok, cnt, cge), jnp.where(ok, cgt, cnt)

        return lax.fori_loop(0, 16, bit_body,
                             (jnp.full((1, tq), -32768, I32), cge0, jnp.zeros((1, tq), F32)))

    p_hi, cge_hi, cgt_hi = radix16(
        hi_scr, kf, jnp.full((1, tq), float(s), F32),
        lambda cand: jnp.where(cand <= NEG_HI, n_skip, 0.0))

    p_hi16 = p_hi.astype(I16)

    def bucket_body(c, carry):
        off = pl.multiple_of(c * kc, kc)
        lo_scr[pl.ds(off, kc), :] = jnp.where(
            hi_scr[pl.ds(off, kc), :] == p_hi16, lo_scr[pl.ds(off, kc), :], jnp.int16(-32768))
        return carry

    lax.fori_loop(0, n_chunks, bucket_body, 0)
    p_lo, cge_lo, cgt_lo = radix16(
        lo_scr, kf - cgt_hi, cge_hi - cgt_hi,
        lambda cand: jnp.where(p_hi == NEG_HI, jnp.where(cand <= NEG_LO, n_skip, 0.0), 0.0))

    thr = p_hi * 65536 + (p_lo + 32768)
    cge = cgt_hi + cge_lo
    cgt = cgt_hi + cgt_lo

    def count_keys(indicator):
        def body(c, cnt):
            off = pl.multiple_of(c * kc, kc)
            return cnt + fold_rows(indicator(key_scr[pl.ds(off, kc), :], off), CNT_ROWS)
        cnt = lax.fori_loop(0, n_chunks, body, jnp.zeros((CNT_ROWS, tq), F32))
        return jnp.sum(cnt, axis=0, keepdims=True)

    has_ties = jnp.max(cge) > kf

    @pl.when(jnp.logical_not(has_ties))
    def _():
        def body(c, carry):
            off = pl.multiple_of(c * kc, kc)
            kk = key_scr[pl.ds(off, kc), :]
            causal_bias = jnp.where(off + key_i <= qry_t, 0.0, NEG_INF)
            bias_scr[pl.ds(off, kc), :] = jnp.where(kk >= thr, causal_bias, NEG_INF)
            return carry
        lax.fori_loop(0, n_chunks, body, 0)

    @pl.when(has_ties)
    def _():
        need = kf - cgt
        n_bits = int(s).bit_length()

        def tie_body(it, jm):
            cand = jm + lax.shift_left(jnp.int32(1), (n_bits - 1) - it)
            g = count_keys(lambda kk, off: jnp.where(
                kk == thr, jnp.where(off + key_i < cand, 1.0, 0.0), 0.0))
            return jnp.where(g < need, cand, jm)

        jm = lax.fori_loop(0, n_bits, tie_body, jnp.zeros((1, tq), I32))

        def body(c, carry):
            off = pl.multiple_of(c * kc, kc)
            kk = key_scr[pl.ds(off, kc), :]
            idx = off + key_i
            causal_bias = jnp.where(idx <= qry_t, 0.0, NEG_INF)
            tie_bias = jnp.where(idx <= jm, causal_bias, NEG_INF)
            bias_scr[pl.ds(off, kc), :] = jnp.where(
                kk > thr, causal_bias, jnp.where(kk == thr, tie_bias, NEG_INF))
            return carry
        lax.fori_loop(0, n_chunks, body, 0)

    q = q_ref[0]
    q_h = [q[:, hh * HEAD_DIM:(hh + 1) * HEAD_DIM] for hh in range(N_HEADS)]

    def masked_logits(c, hh):
        off = pl.multiple_of(c * kc, kc)
        k_c = kv_ref[0, pl.ds(off, kc), 0:HEAD_DIM]
        return _dot_nt(k_c, q_h[hh]) + bias_scr[pl.ds(off, kc), :]

    def attend(m_all):
        def exp_body(c, carry):
            off = pl.multiple_of(c * kc, kc)
            for hh in range(N_HEADS):
                p = jnp.exp2(masked_logits(c, hh) - m_all[hh:hh + 1, :])
                p_scr[pl.ds(off, kc), hh * tq:(hh + 1) * tq] = p.astype(BF16)
            return carry

        lax.fori_loop(0, n_chunks, exp_body, 0)
        acc_scr[...] = jnp.zeros((VT_ROWS, N_HEADS * tq), F32)

        def pv_body(c, carry):
            off = pl.multiple_of(c * kc, kc)
            acc_scr[...] += _dot(vt_ref[0, :, pl.ds(off, kc)], p_scr[pl.ds(off, kc), :])
            return carry

        lax.fori_loop(0, n_chunks, pv_body, 0)

    @pl.when(j == 0)
    def _():
        kf32 = kv_ref[0][:, 0:HEAD_DIM].astype(F32)
        kmax_scr[...] = jnp.full((1, LANES), jnp.max(jnp.sum(kf32 * kf32, axis=-1, keepdims=True)), F32)

    qf32 = q.astype(F32)
    qsq_hi, qsq_lo = _split_bf16(qf32 * qf32)
    qn2 = _dot_nt(hsum_ref[...], qsq_hi) + _dot_nt(hsum_ref[...], qsq_lo)
    attend(jnp.sqrt(qn2 * kmax_scr[0:1, 0:1]) * BOUND_SLACK)

    @pl.when(jnp.min(acc_scr[HEAD_DIM:HEAD_DIM + 1, :]) < MIN_DENOM)
    def _():
        def max_body(c, mparts):
            return tuple(
                jnp.maximum(mparts[hh],
                            jnp.max(masked_logits(c, hh).reshape(kc // 8, 8, tq), axis=0))
                for hh in range(N_HEADS))

        mparts = lax.fori_loop(0, n_chunks, max_body,
                               tuple(jnp.full((8, tq), -3e38, F32) for _ in range(N_HEADS)))
        attend(jnp.concatenate([jnp.max(mp, axis=0, keepdims=True) for mp in mparts], axis=0))

    outs = []
    for hh in range(N_HEADS):
        o = (acc_scr[0:HEAD_DIM, hh * tq:(hh + 1) * tq]
             / acc_scr[HEAD_DIM:HEAD_DIM + 1, hh * tq:(hh + 1) * tq])
        ms = jnp.mean(o * o, axis=0, keepdims=True)
        outs.append(o * lax.rsqrt(ms + EPS))
    out_t = jnp.concatenate(outs, axis=0)
    o_ref[0] = (jnp.transpose(out_t) * g_ref[...]).astype(BF16)


def _attention(q, qi, wt, kv, ki, vt, attn_g, head_sum):
    bsz, s, _ = q.shape
    tq = min(TQ, s)
    kc = min(KC, s)
    k_sel = min(TOPK_MAX, s // 4)
    grid = (bsz, s // tq)
    qtile = lambda w: pl.BlockSpec((1, tq, w), lambda b, j: (b, j, 0))
    seq = lambda w: pl.BlockSpec((1, s, w), lambda b, j: (b, 0, 0))
    return pl.pallas_call(
        functools.partial(_attn_kernel, s=s, k_sel=k_sel, tq=tq, kc=kc),
        grid=grid,
        in_specs=[qtile(ATTN_WIDTH), qtile(IDX_HEADS * IDX_DIM),
                  pl.BlockSpec((1, IDX_HEADS, tq), lambda b, j: (b, 0, j)),
                  seq(2 * HEAD_DIM), seq(LANES),
                  pl.BlockSpec((1, VT_ROWS, s), lambda b, j: (b, 0, 0)),
                  pl.BlockSpec((1, ATTN_WIDTH), lambda b, j: (0, 0)),
                  pl.BlockSpec((N_HEADS, ATTN_WIDTH), lambda b, j: (0, 0))],
        out_specs=qtile(ATTN_WIDTH),
        out_shape=jax.ShapeDtypeStruct((bsz, s, ATTN_WIDTH), BF16),
        scratch_shapes=[pltpu.VMEM((s, tq), I32), pltpu.VMEM((s, tq), I16),
                        pltpu.VMEM((s, tq), I16), pltpu.VMEM((s, tq), F32),
                        pltpu.VMEM((s, N_HEADS * tq), BF16),
                        pltpu.VMEM((VT_ROWS, N_HEADS * tq), F32),
                        pltpu.VMEM((1, LANES), F32)],
        compiler_params=pltpu.CompilerParams(
            dimension_semantics=("arbitrary", "arbitrary"), vmem_limit_bytes=VMEM_LIMIT),
        name="attn",
    )(q, qi, wt, kv, ki, vt, attn_g, head_sum)


def _layer_norm(y, g, b):
    mu = jnp.mean(y, axis=-1, keepdims=True)
    yc = y - mu
    var = jnp.mean(yc * yc, axis=-1, keepdims=True)
    return yc * lax.rsqrt(var + EPS) * g + b


def _post_kernel(x_ref, at_ref, cv_ref, g1_ref, sh2_ref, sc2_ref, g2_ref,
                 wo_ref, l1g_ref, l1b_ref, wg_ref, wu_ref, wd_ref, l2g_ref, l2b_ref, o_ref):
    mix = _dot(at_ref[0], wo_ref[0:ATTN_WIDTH, :]) + _dot(cv_ref[0], wo_ref[ATTN_WIDTH:D_MODEL, :])
    x1 = _layer_norm(DEEPNORM_ALPHA * x_ref[0] + g1_ref[0] * mix, l1g_ref[...], l1b_ref[...])
    h2 = (x1 * (1.0 + sc2_ref[0]) + sh2_ref[0]).astype(BF16)
    gt = _dot(h2, wg_ref[...])
    up = _dot(h2, wu_ref[...])
    hid = (gt * (1.0 / (1.0 + jnp.exp(-gt))) * up).astype(BF16)
    ff = _dot(hid, wd_ref[...])
    o_ref[0] = _layer_norm(DEEPNORM_ALPHA * x1 + g2_ref[0] * ff, l2g_ref[...], l2b_ref[...])


def _post(x, attn, conv, g1, sh2, sc2, g2, w_out, l1g, l1b, w_gate, w_up, w_down, l2g, l2b):
    bsz, s, d = x.shape
    tm = min(TM_PROJ, s)
    grid = (bsz, s // tm)
    tok = lambda w: pl.BlockSpec((1, tm, w), lambda b, i: (b, i, 0))
    vec = pl.BlockSpec((1, 1, d), lambda b, i: (b, 0, 0))
    full = lambda a: pl.BlockSpec(a.shape, lambda b, i: (0,) * a.ndim,
                                  pipeline_mode=pl.Buffered(1))
    return pl.pallas_call(
        _post_kernel,
        grid=grid,
        in_specs=[tok(d), tok(ATTN_WIDTH), tok(CONV_WIDTH), vec, vec, vec, vec,
                  full(w_out), full(l1g), full(l1b), full(w_gate), full(w_up), full(w_down),
                  full(l2g), full(l2b)],
        out_specs=tok(d),
        out_shape=jax.ShapeDtypeStruct((bsz, s, d), F32),
        compiler_params=pltpu.CompilerParams(
            dimension_semantics=("arbitrary", "arbitrary"), vmem_limit_bytes=VMEM_LIMIT),
        name="post",
    )(x, attn, conv, g1, sh2, sc2, g2, w_out, l1g, l1b, w_gate, w_up, w_down, l2g, l2b)


def _front_layout():
    src = np.zeros(FRONT, np.int64)
    used = np.zeros(FRONT, bool)
    partner = np.zeros(FRONT, np.int64)
    freq = np.full(FRONT, ROPE_DIM // 2, np.int64)
    sign = np.zeros(FRONT, np.float32)
    scale = np.ones(FRONT, np.float32)
    half = ROPE_DIM // 2

    def place(dst, col, n_heads, dim, rope, sc):
        for hh in range(n_heads):
            for dd in range(dim):
                lane = dst + hh * dim + dd
                src[lane] = col + hh * dim + dd
                used[lane] = True
                scale[lane] = sc
                if rope and dd < ROPE_DIM:
                    freq[lane] = dd % half
                    if dd < half:
                        partner[lane], sign[lane] = src[lane] + half, -1.0
                    else:
                        partner[lane], sign[lane] = src[lane] - half, 1.0

    c_q, c_k, c_v, c_qi, c_ki, c_wi = 0, 512, 576, 640, 896, 928
    place(OFF_Q, c_q, N_HEADS, HEAD_DIM, True, HEAD_DIM ** -0.5 * LOG2E)
    place(OFF_K, c_k, 1, HEAD_DIM, True, 1.0)
    place(OFF_V, c_v, 1, HEAD_DIM, False, 1.0)
    place(OFF_QI, c_qi, IDX_HEADS, IDX_DIM, True, 1.0)
    place(OFF_KI, c_ki, 1, IDX_DIM, True, 1.0)
    place(OFF_WI, c_wi, 1, IDX_HEADS, False, IDX_HEADS ** -0.5 * IDX_DIM ** -0.5)
    return src, used, partner, freq, sign, scale


def _pack_weights(w_in):
    src, used, partner, _, sign, _ = _front_layout()
    front = jnp.where(used[None, :], w_in[:, src], 0.0)
    part = jnp.where((sign != 0)[None, :], w_in[:, partner], 0.0)
    conv_cols = w_in[:, OFF_WI + IDX_HEADS:]
    return jnp.concatenate([front, part, conv_cols], axis=1).astype(BF16)


def _rope_tables(s):
    _, _, _, freq, sign, scale = _front_layout()
    pos = jnp.arange(s, dtype=F32)
    inv = ROPE_THETA ** (-(jnp.arange(0, ROPE_DIM, 2, dtype=F32) / ROPE_DIM))
    ang = pos[:, None] * inv[None, :]
    cos_ext = jnp.concatenate([jnp.cos(ang), jnp.ones((s, 1), F32)], axis=1)
    sin_ext = jnp.concatenate([jnp.sin(ang), jnp.zeros((s, 1), F32)], axis=1)
    ctab = cos_ext[:, freq] * scale[None, :]
    stab = sin_ext[:, freq] * (sign * scale)[None, :]
    return ctab, stab


def kernel(x, c, w_ada, b_ada, w_in, conv_w, attn_norm_g, conv_norm_g, w_out, ln1_g, ln1_b,
           w_gate, w_up, w_down, ln2_g, ln2_b):
    bsz, s, d = x.shape
    ctab, stab = _rope_tables(s)
    grp = CONV_WIDTH // CONV_GROUPS
    gmat = jnp.asarray(np.kron(np.eye(CONV_GROUPS), np.ones((grp, grp))), BF16)
    for l in range(DEPTH):
        mod = _ada(c, w_ada[l], b_ada[l])
        sh1, sc1, g1, sh2, sc2, g2 = [m.reshape(bsz, 1, d) for m in jnp.split(mod, 6, axis=-1)]
        q, kv, qi, ki, vt, wt, conv = _inproj(
            x, sh1, sc1, _pack_weights(w_in[l]), ctab, stab, conv_w[l],
            conv_norm_g[l].reshape(1, -1), gmat)
        head_sum = jnp.asarray(np.kron(np.eye(N_HEADS), np.ones((1, HEAD_DIM))), BF16)
        attn = _attention(q, qi, wt, kv, ki, vt, attn_norm_g[l].reshape(1, -1), head_sum)
        x = _post(x, attn, conv, g1, sh2, sc2, g2,
                  w_out[l].astype(BF16), ln1_g[l].reshape(1, -1), ln1_b[l].reshape(1, -1),
                  w_gate[l].astype(BF16), w_up[l].astype(BF16), w_down[l].astype(BF16),
                  ln2_g[l].reshape(1, -1), ln2_b[l].reshape(1, -1))
    return x
```

```python
import functools

import numpy as np
import jax
import jax.numpy as jnp
from jax import lax
from jax.experimental import pallas as pl
from jax.experimental.pallas import tpu as pltpu

D_MODEL = 1024
N_HEADS = 8
HEAD_DIM = 64
ATTN_WIDTH = N_HEADS * HEAD_DIM
IDX_HEADS = 8
IDX_DIM = 32
TOPK_MAX = 256
CONV_WIDTH = D_MODEL - ATTN_WIDTH
CONV_GROUPS = 8
CONV_K = 3
ROPE_THETA = 500000.0
ROPE_DIM = HEAD_DIM // 4
D_FF = 2816
DEPTH = 1
DEEPNORM_ALPHA = (2.0 * DEPTH) ** 0.25
EPS = 1e-5
NEG_INF = -1e30

F32 = jnp.float32
BF16 = jnp.bfloat16
I32 = jnp.int32
I16 = jnp.int16

FRONT = 1024
OFF_Q, OFF_K, OFF_V, OFF_QI, OFF_KI, OFF_WI = 0, 512, 576, 640, 896, 928
N_COLS = FRONT + 3 * CONV_WIDTH

TM_PROJ = 512
TQ = 256
KC = 512
CNT_ROWS = 64
VT_ROWS = 80
LOG2E = 1.4426950408889634
BOUND_SLACK = 1.002
MIN_DENOM = 2.0 ** -60
LANES = 128
VMEM_LIMIT = 56 * 1024 * 1024


def _sortable_key(x):
    bits = lax.bitcast_convert_type(x, I32)
    return bits ^ (lax.shift_right_arithmetic(bits, 31) & jnp.int32(0x7FFFFFFF))


_NEG_BITS = int(np.array(NEG_INF, np.float32).view(np.int32))
NEG_KEY = _NEG_BITS ^ ((_NEG_BITS >> 31) & 0x7FFFFFFF)
NEG_HI = NEG_KEY >> 16
NEG_LO = (NEG_KEY & 0xFFFF) - 32768


def _dot(a, b):
    return jnp.dot(a, b, preferred_element_type=F32)


def _dot_nt(a, b):
    return lax.dot_general(a, b, (((1,), (1,)), ((), ())), preferred_element_type=F32)


def _split_bf16(a):
    hi = a.astype(BF16)
    lo = (a - hi.astype(F32)).astype(BF16)
    return hi, lo


def _ada_kernel(c_ref, w_ref, b_ref, o_ref):
    c = c_ref[...]
    ca = c * (1.0 / (1.0 + jnp.exp(-c)))
    ca_hi, ca_lo = _split_bf16(ca)
    w_hi, w_lo = _split_bf16(w_ref[...])
    acc = _dot(ca_hi, w_hi) + _dot(ca_hi, w_lo) + _dot(ca_lo, w_hi)
    o_ref[...] = acc + b_ref[...]


def _ada(c, w_ada, b_ada):
    bsz, d = c.shape
    n = w_ada.shape[1]
    blk = 1024
    return pl.pallas_call(
        _ada_kernel,
        grid=(n // blk,),
        in_specs=[
            pl.BlockSpec((bsz, d), lambda j: (0, 0)),
            pl.BlockSpec((d, blk), lambda j: (0, j)),
            pl.BlockSpec((1, blk), lambda j: (0, j)),
        ],
        out_specs=pl.BlockSpec((bsz, blk), lambda j: (0, j)),
        out_shape=jax.ShapeDtypeStruct((bsz, n), F32),
        name="ada",
    )(c, w_ada, b_ada.reshape(1, n))


def _inproj_kernel(x_ref, sh_ref, sc_ref, w_ref, ct_ref, su_ref, sd_ref, cw_ref, cg_ref, gm_ref,
                   q_ref, kv_ref, qi_ref, ki_ref, vt_ref, wt_ref, cv_ref, ubuf, halo, *, tm):
    i = pl.program_id(0)
    b = pl.program_id(1)
    h = (x_ref[0] * (1.0 + sc_ref[0]) + sh_ref[0]).astype(BF16)

    front = _dot(h, w_ref[:, 0:FRONT])
    blocks = []
    for blk in range(FRONT // LANES):
        lanes = slice(blk * LANES, (blk + 1) * LANES)
        f = front[:, lanes]
        up = pltpu.roll(f, LANES - ROPE_DIM // 2, axis=1)
        dn = pltpu.roll(f, ROPE_DIM // 2, axis=1)
        blocks.append(f * ct_ref[:, lanes] + up * su_ref[:, lanes] + dn * sd_ref[:, lanes])
    roped = jnp.concatenate(blocks, axis=1)
    q_ref[0] = roped[:, OFF_Q:OFF_K].astype(BF16)
    kv = roped[:, OFF_K:OFF_QI]
    kv_ref[0] = kv.astype(BF16)
    qi_ref[0] = roped[:, OFF_QI:OFF_KI].astype(BF16)
    kw = roped[:, OFF_KI:FRONT]
    ki_ref[0] = kw.astype(BF16)
    vt_ref[0, 0:HEAD_DIM, :] = jnp.transpose(kv)[HEAD_DIM:2 * HEAD_DIM, :].astype(BF16)
    vt_ref[0, HEAD_DIM:VT_ROWS, :] = jnp.ones((VT_ROWS - HEAD_DIM, tm), BF16)
    wt_ref[0] = jnp.transpose(kw)[OFF_WI - OFF_KI:OFF_WI - OFF_KI + IDX_HEADS, :]

    cp = _dot(h, w_ref[:, FRONT:N_COLS])
    gate_b = cp[:, 0:CONV_WIDTH]
    u = cp[:, CONV_WIDTH:2 * CONV_WIDTH] * cp[:, 2 * CONV_WIDTH:3 * CONV_WIDTH]

    @pl.when(i == 0)
    def _():
        ubuf[0:8, :] = jnp.zeros((8, CONV_WIDTH), F32)

    @pl.when(i > 0)
    def _():
        ubuf[0:8, :] = halo[b]

    ubuf[8:8 + tm, :] = u
    conv = (cw_ref[0:1, :] * ubuf[6:6 + tm, :] + cw_ref[1:2, :] * ubuf[7:7 + tm, :]
            + cw_ref[2:3, :] * u)
    y = gate_b * conv
    y2_hi, y2_lo = _split_bf16(y * y)
    ssq = _dot(y2_hi, gm_ref[...]) + _dot(y2_lo, gm_ref[...])
    yn = y * lax.rsqrt(ssq * (1.0 / (CONV_WIDTH // CONV_GROUPS)) + EPS) * cg_ref[...]
    cv_ref[0] = yn.astype(BF16)
    halo[b] = ubuf[tm:tm + 8, :]


def _inproj(x, sh1, sc1, w_all, ctab, sup, sdn, conv_w, conv_g, gmat):
    bsz, s, d = x.shape
    tm = min(TM_PROJ, s)
    grid = (s // tm, bsz)
    tok = lambda w: pl.BlockSpec((1, tm, w), lambda i, b: (b, i, 0))
    tok_t = lambda r: pl.BlockSpec((1, r, tm), lambda i, b: (b, 0, i))
    vec = pl.BlockSpec((1, 1, d), lambda i, b: (b, 0, 0))
    full = lambda a: pl.BlockSpec(a.shape, lambda i, b: (0,) * a.ndim)
    table = pl.BlockSpec((tm, FRONT), lambda i, b: (i, 0))
    return pl.pallas_call(
        functools.partial(_inproj_kernel, tm=tm),
        grid=grid,
        in_specs=[tok(d), vec, vec, full(w_all), table, table, table,
                  full(conv_w), full(conv_g), full(gmat)],
        out_specs=[tok(ATTN_WIDTH), tok(2 * HEAD_DIM), tok(IDX_HEADS * IDX_DIM),
                   tok(LANES), tok_t(VT_ROWS), tok_t(IDX_HEADS), tok(CONV_WIDTH)],
        out_shape=[
            jax.ShapeDtypeStruct((bsz, s, ATTN_WIDTH), BF16),
            jax.ShapeDtypeStruct((bsz, s, 2 * HEAD_DIM), BF16),
            jax.ShapeDtypeStruct((bsz, s, IDX_HEADS * IDX_DIM), BF16),
            jax.ShapeDtypeStruct((bsz, s, LANES), BF16),
            jax.ShapeDtypeStruct((bsz, VT_ROWS, s), BF16),
            jax.ShapeDtypeStruct((bsz, IDX_HEADS, s), F32),
            jax.ShapeDtypeStruct((bsz, s, CONV_WIDTH), BF16),
        ],
        scratch_shapes=[pltpu.VMEM((tm + 8, CONV_WIDTH), F32),
                        pltpu.VMEM((bsz, 8, CONV_WIDTH), F32)],
        compiler_params=pltpu.CompilerParams(
            dimension_semantics=("arbitrary", "arbitrary"), vmem_limit_bytes=VMEM_LIMIT),
        name="inproj",
    )(x, sh1, sc1, w_all, ctab, sup, sdn, conv_w, conv_g, gmat)


def _attn_kernel(q_ref, qi_ref, wt_ref, kv_ref, ki_ref, vt_ref, g_ref, hsum_ref, o_ref,
                 key_scr, hi_scr, lo_scr, bias_scr, p_scr, acc_scr, kmax_scr,
                 *, s, k_sel, tq, kc):
    j = pl.program_id(1)
    t0 = j * tq
    n_chunks = (t0 + tq + kc - 1) // kc
    n_skip = (s - n_chunks * kc).astype(F32)
    kf = float(k_sel)

    key_i = lax.broadcasted_iota(I32, (kc, tq), 0)
    qry_t = t0 + lax.broadcasted_iota(I32, (kc, tq), 1)

    def for_chunks(body):
        def pair(i, carry):
            body(2 * i)
            body(2 * i + 1)
            return carry

        lax.fori_loop(0, n_chunks // 2, pair, 0)

        @pl.when(n_chunks % 2 == 1)
        def _():
            body(n_chunks - 1)

    def fold_rows(a, rows):
        return jnp.sum(a.reshape(kc // rows, rows, tq), axis=0)

    qi = qi_ref[0]
    qi_h = [qi[:, hh * IDX_DIM:(hh + 1) * IDX_DIM] for hh in range(IDX_HEADS)]
    wt = wt_ref[0]
    wt_h = [wt[hh:hh + 1, :] for hh in range(IDX_HEADS)]

    def score_chunk(c):
        off = pl.multiple_of(c * kc, kc)
        ki_c = ki_ref[0, pl.ds(off, kc), 0:IDX_DIM]
        acc = jnp.zeros((kc, tq), F32)
        for hh in range(IDX_HEADS):
            acc = acc + jnp.maximum(_dot_nt(ki_c, qi_h[hh]), 0.0) * wt_h[hh]
        acc = jnp.where(off + key_i <= qry_t, acc, NEG_INF)
        key = _sortable_key(acc)
        key_scr[pl.ds(off, kc), :] = key
        hi_scr[pl.ds(off, kc), :] = lax.shift_right_arithmetic(key, 16).astype(I16)
        lo_scr[pl.ds(off, kc), :] = ((key & 0xFFFF) - 32768).astype(I16)

    for_chunks(score_chunk)

    def count16(scr, cand):
        cand16 = cand.astype(I16)

        def body(c, cnt):
            off = pl.multiple_of(c * kc, kc)
            ind = jnp.where(scr[pl.ds(off, kc), :] >= cand16, jnp.int16(1), jnp.int16(0))
            parts = [ind[r * CNT_ROWS:(r + 1) * CNT_ROWS, :] for r in range(kc // CNT_ROWS)]
            while len(parts) > 1:
                parts = [parts[i] + parts[i + 1] for i in range(0, len(parts), 2)]
            return cnt + parts[0]

        cnt = lax.fori_loop(0, n_chunks, body, jnp.zeros((CNT_ROWS, tq), I16))
        return jnp.sum(cnt.astype(F32), axis=0, keepdims=True)

    def radix16(scr, k_need, cge0, skipped_ge):
        def bit_body(it, carry):
            prefix, cge, cgt = carry
            cand = prefix + lax.shift_left(jnp.int32(1), 15 - it)
            cnt = count16(scr, cand) + skipped_ge(cand)
            ok = cnt >= k_need
            return jnp.where(ok, cand, prefix), jnp.where(ok, cnt, cge), jnp.where(ok, cgt, cnt)

        return lax.fori_loop(0, 16, bit_body,
                             (jnp.full((1, tq), -32768, I32), cge0, jnp.zeros((1, tq), F32)))

    p_hi, cge_hi, cgt_hi = radix16(
        hi_scr, kf, jnp.full((1, tq), float(s), F32),
        lambda cand: jnp.where(cand <= NEG_HI, n_skip, 0.0))

    p_hi16 = p_hi.astype(I16)

    def bucket_body(c, carry):
        off = pl.multiple_of(c * kc, kc)
        lo_scr[pl.ds(off, kc), :] = jnp.where(
            hi_scr[pl.ds(off, kc), :] == p_hi16, lo_scr[pl.ds(off, kc), :], jnp.int16(-32768))
        return carry

    lax.fori_loop(0, n_chunks, bucket_body, 0)
    p_lo, cge_lo, cgt_lo = radix16(
        lo_scr, kf - cgt_hi, cge_hi - cgt_hi,
        lambda cand: jnp.where(p_hi == NEG_HI, jnp.where(cand <= NEG_LO, n_skip, 0.0), 0.0))

    thr = p_hi * 65536 + (p_lo + 32768)
    cge = cgt_hi + cge_lo
    cgt = cgt_hi + cgt_lo

    def count_keys(indicator):
        def body(c, cnt):
            off = pl.multiple_of(c * kc, kc)
            return cnt + fold_rows(indicator(key_scr[pl.ds(off, kc), :], off), CNT_ROWS)
        cnt = lax.fori_loop(0, n_chunks, body, jnp.zeros((CNT_ROWS, tq), F32))
        return jnp.sum(cnt, axis=0, keepdims=True)

    has_ties = jnp.max(cge) > kf

    @pl.when(jnp.logical_not(has_ties))
    def _():
        def body(c, carry):
            off = pl.multiple_of(c * kc, kc)
            kk = key_scr[pl.ds(off, kc), :]
            causal_bias = jnp.where(off + key_i <= qry_t, 0.0, NEG_INF)
            bias_scr[pl.ds(off, kc), :] = jnp.where(kk >= thr, causal_bias, NEG_INF)
            return carry
        lax.fori_loop(0, n_chunks, body, 0)

    @pl.when(has_ties)
    def _():
        need = kf - cgt
        n_bits = int(s).bit_length()

        def tie_body(it, jm):
            cand = jm + lax.shift_left(jnp.int32(1), (n_bits - 1) - it)
            g = count_keys(lambda kk, off: jnp.where(
                kk == thr, jnp.where(off + key_i < cand, 1.0, 0.0), 0.0))
            return jnp.where(g < need, cand, jm)

        jm = lax.fori_loop(0, n_bits, tie_body, jnp.zeros((1, tq), I32))

        def body(c, carry):
            off = pl.multiple_of(c * kc, kc)
            kk = key_scr[pl.ds(off, kc), :]
            idx = off + key_i
            causal_bias = jnp.where(idx <= qry_t, 0.0, NEG_INF)
            tie_bias = jnp.where(idx <= jm, causal_bias, NEG_INF)
            bias_scr[pl.ds(off, kc), :] = jnp.where(
                kk > thr, causal_bias, jnp.where(kk == thr, tie_bias, NEG_INF))
            return carry
        lax.fori_loop(0, n_chunks, body, 0)

    q = q_ref[0]
    q_h = [q[:, hh * HEAD_DIM:(hh + 1) * HEAD_DIM] for hh in range(N_HEADS)]

    def masked_logits(c, hh):
        off = pl.multiple_of(c * kc, kc)
        k_c = kv_ref[0, pl.ds(off, kc), 0:HEAD_DIM]
        return _dot_nt(k_c, q_h[hh]) + bias_scr[pl.ds(off, kc), :]

    def attend(m_all):
        def exp_chunk(c):
            off = pl.multiple_of(c * kc, kc)
            for hh in range(N_HEADS):
                p = jnp.exp2(masked_logits(c, hh) - m_all[hh:hh + 1, :])
                p_scr[pl.ds(off, kc), hh * tq:(hh + 1) * tq] = p.astype(BF16)

        for_chunks(exp_chunk)
        acc_scr[...] = jnp.zeros((VT_ROWS, N_HEADS * tq), F32)

        def pv_chunk(c):
            off = pl.multiple_of(c * kc, kc)
            acc_scr[...] += _dot(vt_ref[0, :, pl.ds(off, kc)], p_scr[pl.ds(off, kc), :])

        for_chunks(pv_chunk)

    @pl.when(j == 0)
    def _():
        kf32 = kv_ref[0][:, 0:HEAD_DIM].astype(F32)
        kmax_scr[...] = jnp.full((1, LANES), jnp.max(jnp.sum(kf32 * kf32, axis=-1, keepdims=True)), F32)

    qf32 = q.astype(F32)
    qsq_hi, qsq_lo = _split_bf16(qf32 * qf32)
    qn2 = _dot_nt(hsum_ref[...], qsq_hi) + _dot_nt(hsum_ref[...], qsq_lo)
    attend(jnp.sqrt(qn2 * kmax_scr[0:1, 0:1]) * BOUND_SLACK)

    @pl.when(jnp.min(acc_scr[HEAD_DIM:HEAD_DIM + 1, :]) < MIN_DENOM)
    def _():
        def max_body(c, mparts):
            return tuple(
                jnp.maximum(mparts[hh],
                            jnp.max(masked_logits(c, hh).reshape(kc // 8, 8, tq), axis=0))
                for hh in range(N_HEADS))

        mparts = lax.fori_loop(0, n_chunks, max_body,
                               tuple(jnp.full((8, tq), -3e38, F32) for _ in range(N_HEADS)))
        attend(jnp.concatenate([jnp.max(mp, axis=0, keepdims=True) for mp in mparts], axis=0))

    outs = []
    for hh in range(N_HEADS):
        o = (acc_scr[0:HEAD_DIM, hh * tq:(hh + 1) * tq]
             / acc_scr[HEAD_DIM:HEAD_DIM + 1, hh * tq:(hh + 1) * tq])
        ms = jnp.mean(o * o, axis=0, keepdims=True)
        outs.append(o * lax.rsqrt(ms + EPS))
    out_t = jnp.concatenate(outs, axis=0)
    o_ref[0] = (jnp.transpose(out_t) * g_ref[...]).astype(BF16)


def _attention(q, qi, wt, kv, ki, vt, attn_g, head_sum):
    bsz, s, _ = q.shape
    tq = min(TQ, s)
    kc = min(KC, s)
    k_sel = min(TOPK_MAX, s // 4)
    grid = (bsz, s // tq)
    qtile = lambda w: pl.BlockSpec((1, tq, w), lambda b, j: (b, j, 0))
    seq = lambda w: pl.BlockSpec((1, s, w), lambda b, j: (b, 0, 0))
    return pl.pallas_call(
        functools.partial(_attn_kernel, s=s, k_sel=k_sel, tq=tq, kc=kc),
        grid=grid,
        in_specs=[qtile(ATTN_WIDTH), qtile(IDX_HEADS * IDX_DIM),
                  pl.BlockSpec((1, IDX_HEADS, tq), lambda b, j: (b, 0, j)),
                  seq(2 * HEAD_DIM), seq(LANES),
                  pl.BlockSpec((1, VT_ROWS, s), lambda b, j: (b, 0, 0)),
                  pl.BlockSpec((1, ATTN_WIDTH), lambda b, j: (0, 0)),
                  pl.BlockSpec((N_HEADS, ATTN_WIDTH), lambda b, j: (0, 0))],
        out_specs=qtile(ATTN_WIDTH),
        out_shape=jax.ShapeDtypeStruct((bsz, s, ATTN_WIDTH), BF16),
        scratch_shapes=[pltpu.VMEM((s, tq), I32), pltpu.VMEM((s, tq), I16),
                        pltpu.VMEM((s, tq), I16), pltpu.VMEM((s, tq), F32),
                        pltpu.VMEM((s, N_HEADS * tq), BF16),
                        pltpu.VMEM((VT_ROWS, N_HEADS * tq), F32),
                        pltpu.VMEM((1, LANES), F32)],
        compiler_params=pltpu.CompilerParams(
            dimension_semantics=("arbitrary", "arbitrary"), vmem_limit_bytes=VMEM_LIMIT),
        name="attn",
    )(q, qi, wt, kv, ki, vt, attn_g, head_sum)


def _layer_norm(y, g, b):
    mu = jnp.mean(y, axis=-1, keepdims=True)
    yc = y - mu
    var = jnp.mean(yc * yc, axis=-1, keepdims=True)
    return yc * lax.rsqrt(var + EPS) * g + b


def _post_kernel(x_ref, at_ref, cv_ref, g1_ref, sh2_ref, sc2_ref, g2_ref,
                 wo_ref, l1g_ref, l1b_ref, wg_ref, wu_ref, wd_ref, l2g_ref, l2b_ref, o_ref):
    mix = _dot(at_ref[0], wo_ref[0:ATTN_WIDTH, :]) + _dot(cv_ref[0], wo_ref[ATTN_WIDTH:D_MODEL, :])
    x1 = _layer_norm(DEEPNORM_ALPHA * x_ref[0] + g1_ref[0] * mix, l1g_ref[...], l1b_ref[...])
    h2 = (x1 * (1.0 + sc2_ref[0]) + sh2_ref[0]).astype(BF16)
    gt = _dot(h2, wg_ref[...])
    up = _dot(h2, wu_ref[...])
    hid = (gt * (1.0 / (1.0 + jnp.exp(-gt))) * up).astype(BF16)
    ff = _dot(hid, wd_ref[...])
    o_ref[0] = _layer_norm(DEEPNORM_ALPHA * x1 + g2_ref[0] * ff, l2g_ref[...], l2b_ref[...])


def _post(x, attn, conv, g1, sh2, sc2, g2, w_out, l1g, l1b, w_gate, w_up, w_down, l2g, l2b):
    bsz, s, d = x.shape
    tm = min(TM_PROJ, s)
    grid = (bsz, s // tm)
    tok = lambda w: pl.BlockSpec((1, tm, w), lambda b, i: (b, i, 0))
    vec = pl.BlockSpec((1, 1, d), lambda b, i: (b, 0, 0))
    full = lambda a: pl.BlockSpec(a.shape, lambda b, i: (0,) * a.ndim,
                                  pipeline_mode=pl.Buffered(1))
    return pl.pallas_call(
        _post_kernel,
        grid=grid,
        in_specs=[tok(d), tok(ATTN_WIDTH), tok(CONV_WIDTH), vec, vec, vec, vec,
                  full(w_out), full(l1g), full(l1b), full(w_gate), full(w_up), full(w_down),
                  full(l2g), full(l2b)],
        out_specs=tok(d),
        out_shape=jax.ShapeDtypeStruct((bsz, s, d), F32),
        compiler_params=pltpu.CompilerParams(
            dimension_semantics=("arbitrary", "arbitrary"), vmem_limit_bytes=VMEM_LIMIT),
        name="post",
    )(x, attn, conv, g1, sh2, sc2, g2, w_out, l1g, l1b, w_gate, w_up, w_down, l2g, l2b)


def _pack_weights(w_in):
    n_front = OFF_WI + IDX_HEADS
    front = jnp.pad(w_in[:, :n_front], ((0, 0), (0, FRONT - n_front)))
    return jnp.concatenate([front, w_in[:, n_front:]], axis=1).astype(BF16)


def _rope_tables(s):
    pos = jnp.arange(s, dtype=F32)
    inv = ROPE_THETA ** (-(jnp.arange(0, ROPE_DIM, 2, dtype=F32) / ROPE_DIM))
    ang = pos[:, None] * inv[None, :]
    cos, sin = jnp.cos(ang), jnp.sin(ang)
    half = ROPE_DIM // 2

    def region(n_heads, dim, rope, scale):
        if rope:
            own = jnp.concatenate([cos, cos, jnp.ones((s, dim - ROPE_DIM), F32)], axis=1)
            up = jnp.concatenate([-sin, jnp.zeros((s, dim - half), F32)], axis=1)
            dn = jnp.concatenate([jnp.zeros((s, half), F32), sin,
                                  jnp.zeros((s, dim - ROPE_DIM), F32)], axis=1)
        else:
            own = jnp.ones((s, dim), F32)
            up = dn = jnp.zeros((s, dim), F32)
        return tuple(jnp.tile(t, (1, n_heads)) * scale for t in (own, up, dn))

    regions = [
        region(N_HEADS, HEAD_DIM, True, HEAD_DIM ** -0.5 * LOG2E),
        region(1, HEAD_DIM, True, 1.0),
        region(1, HEAD_DIM, False, 1.0),
        region(IDX_HEADS, IDX_DIM, True, 1.0),
        region(1, IDX_DIM, True, 1.0),
        region(1, IDX_HEADS, False, IDX_HEADS ** -0.5 * IDX_DIM ** -0.5),
        region(1, FRONT - OFF_WI - IDX_HEADS, False, 1.0),
    ]
    return tuple(jnp.concatenate([r[t] for r in regions], axis=1) for t in range(3))


def kernel(x, c, w_ada, b_ada, w_in, conv_w, attn_norm_g, conv_norm_g, w_out, ln1_g, ln1_b,
           w_gate, w_up, w_down, ln2_g, ln2_b):
    bsz, s, d = x.shape
    ctab, sup, sdn = _rope_tables(s)
    grp = CONV_WIDTH // CONV_GROUPS
    gmat = jnp.asarray(np.kron(np.eye(CONV_GROUPS), np.ones((grp, grp))), BF16)
    for l in range(DEPTH):
        mod = _ada(c, w_ada[l], b_ada[l])
        sh1, sc1, g1, sh2, sc2, g2 = [m.reshape(bsz, 1, d) for m in jnp.split(mod, 6, axis=-1)]
        q, kv, qi, ki, vt, wt, conv = _inproj(
            x, sh1, sc1, _pack_weights(w_in[l]), ctab, sup, sdn, conv_w[l],
            conv_norm_g[l].reshape(1, -1), gmat)
        head_sum = jnp.asarray(np.kron(np.eye(N_HEADS), np.ones((1, HEAD_DIM))), BF16)
        attn = _attention(q, qi, wt, kv, ki, vt, attn_norm_g[l].reshape(1, -1), head_sum)
        x = _post(x, attn, conv, g1, sh2, sc2, g2,
                  w_out[l].astype(BF16), ln1_g[l].reshape(1, -1), ln1_b[l].reshape(1, -1),
                  w_gate[l].astype(BF16), w_up[l].astype(BF16), w_down[l].astype(BF16),
                  ln2_g[l].reshape(1, -1), ln2_b[l].reshape(1, -1))
    return x
```

```python
import functools

import numpy as np
import jax
import jax.numpy as jnp
from jax import lax
from jax.experimental import pallas as pl
from jax.experimental.pallas import tpu as pltpu

D_MODEL = 1024
N_HEADS = 8
HEAD_DIM = 64
ATTN_WIDTH = N_HEADS * HEAD_DIM
IDX_HEADS = 8
IDX_DIM = 32
TOPK_MAX = 256
CONV_WIDTH = D_MODEL - ATTN_WIDTH
CONV_GROUPS = 8
CONV_K = 3
ROPE_THETA = 500000.0
ROPE_DIM = HEAD_DIM // 4
D_FF = 2816
DEPTH = 1
DEEPNORM_ALPHA = (2.0 * DEPTH) ** 0.25
EPS = 1e-5
NEG_INF = -1e30

F32 = jnp.float32
BF16 = jnp.bfloat16
I32 = jnp.int32

FRONT = 1024
OFF_Q, OFF_K, OFF_V, OFF_QI, OFF_KI, OFF_WI = 0, 512, 576, 640, 896, 928
N_COLS = FRONT + 3 * CONV_WIDTH
ROPE_PATTERN = (0, 0, 0, 0, 1, 2, 2, 3)

TM_PROJ = 512
TQ = 512
PV_HEADS = 4
KC = 512
CNT_SLAB = 64
CNT_ROWS = 32
VT_ROWS = 80
LOG2E = 1.4426950408889634
BOUND_SLACK = 1.002
MIN_DENOM = 2.0 ** -60
LANES = 128
VMEM_LIMIT = 56 * 1024 * 1024


def _key_to_float(key):
    bits = key ^ (lax.shift_right_arithmetic(key, 31) & jnp.int32(0x7FFFFFFF))
    return lax.bitcast_convert_type(bits, F32)


def _dot(a, b):
    return jnp.dot(a, b, preferred_element_type=F32)


def _dot_nt(a, b):
    return lax.dot_general(a, b, (((1,), (1,)), ((), ())), preferred_element_type=F32)


def _split_bf16(a):
    hi = a.astype(BF16)
    lo = (a - hi.astype(F32)).astype(BF16)
    return hi, lo


def _ada_kernel(c_ref, w_ref, b_ref, o_ref):
    c = c_ref[...]
    ca = c * (1.0 / (1.0 + jnp.exp(-c)))
    ca_hi, ca_lo = _split_bf16(ca)
    w_hi, w_lo = _split_bf16(w_ref[...])
    acc = _dot(ca_hi, w_hi) + _dot(ca_hi, w_lo) + _dot(ca_lo, w_hi)
    o_ref[...] = acc + b_ref[...]


def _ada(c, w_ada, b_ada):
    bsz, d = c.shape
    n = w_ada.shape[1]
    blk = 1024
    return pl.pallas_call(
        _ada_kernel,
        grid=(n // blk,),
        in_specs=[
            pl.BlockSpec((bsz, d), lambda j: (0, 0)),
            pl.BlockSpec((d, blk), lambda j: (0, j)),
            pl.BlockSpec((1, blk), lambda j: (0, j)),
        ],
        out_specs=pl.BlockSpec((bsz, blk), lambda j: (0, j)),
        out_shape=jax.ShapeDtypeStruct((bsz, n), F32),
        name="ada",
    )(c, w_ada, b_ada.reshape(1, n))


def _inproj_kernel(x_ref, sh_ref, sc_ref, w_ref, ct_ref, su_ref, sd_ref, cw_ref, cg_ref, gm_ref,
                   q_ref, kv_ref, qi_ref, ki_ref, vt_ref, wt_ref, cv_ref, ubuf, halo, *, tm):
    i = pl.program_id(0)
    b = pl.program_id(1)
    h = (x_ref[0] * (1.0 + sc_ref[0]) + sh_ref[0]).astype(BF16)

    front = _dot(h, w_ref[:, 0:FRONT])
    blocks = []
    for blk in range(FRONT // LANES):
        lanes = slice(blk * LANES, (blk + 1) * LANES)
        f = front[:, lanes]
        up = pltpu.roll(f, LANES - ROPE_DIM // 2, axis=1)
        dn = pltpu.roll(f, ROPE_DIM // 2, axis=1)
        pat = slice(ROPE_PATTERN[blk] * LANES, (ROPE_PATTERN[blk] + 1) * LANES)
        blocks.append(f * ct_ref[:, pat] + up * su_ref[:, pat] + dn * sd_ref[:, pat])
    roped = jnp.concatenate(blocks, axis=1)
    q_ref[0] = roped[:, OFF_Q:OFF_K].astype(BF16)
    kv = roped[:, OFF_K:OFF_QI]
    kv_ref[0] = kv.astype(BF16)
    qi_ref[0] = roped[:, OFF_QI:OFF_KI].astype(BF16)
    kw = roped[:, OFF_KI:FRONT]
    ki_ref[0] = kw.astype(BF16)
    vt_ref[0, 0:HEAD_DIM, :] = jnp.transpose(kv)[HEAD_DIM:2 * HEAD_DIM, :].astype(BF16)
    vt_ref[0, HEAD_DIM:VT_ROWS, :] = jnp.ones((VT_ROWS - HEAD_DIM, tm), BF16)
    wt_ref[0] = jnp.transpose(kw)[OFF_WI - OFF_KI:OFF_WI - OFF_KI + IDX_HEADS, :]

    cp = _dot(h, w_ref[:, FRONT:N_COLS])
    gate_b = cp[:, 0:CONV_WIDTH]
    u = cp[:, CONV_WIDTH:2 * CONV_WIDTH] * cp[:, 2 * CONV_WIDTH:3 * CONV_WIDTH]

    @pl.when(i == 0)
    def _():
        ubuf[0:8, :] = jnp.zeros((8, CONV_WIDTH), F32)

    @pl.when(i > 0)
    def _():
        ubuf[0:8, :] = halo[b]

    ubuf[8:8 + tm, :] = u
    conv = (cw_ref[0:1, :] * ubuf[6:6 + tm, :] + cw_ref[1:2, :] * ubuf[7:7 + tm, :]
            + cw_ref[2:3, :] * u)
    y = gate_b * conv
    y2_hi, y2_lo = _split_bf16(y * y)
    ssq = _dot(y2_hi, gm_ref[...]) + _dot(y2_lo, gm_ref[...])
    yn = y * lax.rsqrt(ssq * (1.0 / (CONV_WIDTH // CONV_GROUPS)) + EPS) * cg_ref[...]
    cv_ref[0] = yn.astype(BF16)
    halo[b] = ubuf[tm:tm + 8, :]


def _inproj(x, sh1, sc1, w_all, ctab, sup, sdn, conv_w, conv_g, gmat):
    bsz, s, d = x.shape
    tm = min(TM_PROJ, s)
    grid = (s // tm, bsz)
    tok = lambda w: pl.BlockSpec((1, tm, w), lambda i, b: (b, i, 0))
    tok_t = lambda r: pl.BlockSpec((1, r, tm), lambda i, b: (b, 0, i))
    vec = pl.BlockSpec((1, 1, d), lambda i, b: (b, 0, 0))
    full = lambda a: pl.BlockSpec(a.shape, lambda i, b: (0,) * a.ndim)
    table = pl.BlockSpec((tm, ctab.shape[1]), lambda i, b: (i, 0))
    return pl.pallas_call(
        functools.partial(_inproj_kernel, tm=tm),
        grid=grid,
        in_specs=[tok(d), vec, vec, full(w_all), table, table, table,
                  full(conv_w), full(conv_g), full(gmat)],
        out_specs=[tok(ATTN_WIDTH), tok(2 * HEAD_DIM), tok(IDX_HEADS * IDX_DIM),
                   tok(LANES), tok_t(VT_ROWS), tok_t(IDX_HEADS), tok(CONV_WIDTH)],
        out_shape=[
            jax.ShapeDtypeStruct((bsz, s, ATTN_WIDTH), BF16),
            jax.ShapeDtypeStruct((bsz, s, 2 * HEAD_DIM), BF16),
            jax.ShapeDtypeStruct((bsz, s, IDX_HEADS * IDX_DIM), BF16),
            jax.ShapeDtypeStruct((bsz, s, LANES), BF16),
            jax.ShapeDtypeStruct((bsz, VT_ROWS, s), BF16),
            jax.ShapeDtypeStruct((bsz, IDX_HEADS, s), F32),
            jax.ShapeDtypeStruct((bsz, s, CONV_WIDTH), BF16),
        ],
        scratch_shapes=[pltpu.VMEM((tm + 8, CONV_WIDTH), F32),
                        pltpu.VMEM((bsz, 8, CONV_WIDTH), F32)],
        compiler_params=pltpu.CompilerParams(
            dimension_semantics=("arbitrary", "arbitrary"), vmem_limit_bytes=VMEM_LIMIT),
        name="inproj",
    )(x, sh1, sc1, w_all, ctab, sup, sdn, conv_w, conv_g, gmat)


def _attn_kernel(q_ref, qi_ref, wt_ref, kv_ref, ki_ref, vt_ref, g_ref, hsum_ref, o_ref,
                 score_scr, bias_scr, p_scr, acc_scr, kmax_scr,
                 *, s, k_sel, tq, kc):
    j = pl.program_id(1)
    t0 = j * tq
    n_chunks = (t0 + tq + kc - 1) // kc
    n_skip = (s - n_chunks * kc).astype(F32)
    kf = float(k_sel)

    key_i = lax.broadcasted_iota(I32, (kc, tq), 0)
    qry_t = t0 + lax.broadcasted_iota(I32, (kc, tq), 1)

    def for_chunks(body):
        def pair(i, carry):
            body(2 * i)
            body(2 * i + 1)
            return carry

        lax.fori_loop(0, n_chunks // 2, pair, 0)

        @pl.when(n_chunks % 2 == 1)
        def _():
            body(n_chunks - 1)

    def fold_rows(a, rows):
        return jnp.sum(a.reshape(kc // rows, rows, tq), axis=0)

    qi = qi_ref[0]
    qi_h = [qi[:, hh * IDX_DIM:(hh + 1) * IDX_DIM] for hh in range(IDX_HEADS)]
    wt = wt_ref[0]
    wt_h = [wt[hh:hh + 1, :] for hh in range(IDX_HEADS)]

    def score_chunk(c):
        off = pl.multiple_of(c * kc, kc)
        ki_c = ki_ref[0, pl.ds(off, kc), 0:IDX_DIM]
        acc = jnp.zeros((kc, tq), F32)
        for hh in range(IDX_HEADS):
            acc = acc + jnp.maximum(_dot_nt(ki_c, qi_h[hh]), 0.0) * wt_h[hh]
        score_scr[pl.ds(off, kc), :] = jnp.where(off + key_i <= qry_t, acc, NEG_INF)

    for_chunks(score_chunk)

    def count_ge(cand_f):
        def body(c, cnt):
            off = pl.multiple_of(c * kc, kc)
            for r0 in range(0, kc, CNT_SLAB):
                ind = jnp.where(score_scr[pl.ds(off + r0, CNT_SLAB), :] >= cand_f, 1.0, 0.0)
                parts = [ind[r:r + CNT_ROWS, :] for r in range(0, CNT_SLAB, CNT_ROWS)]
                while len(parts) > 1:
                    parts = [parts[i] + parts[i + 1] for i in range(0, len(parts), 2)]
                cnt = cnt + parts[0]
            return cnt

        cnt = lax.fori_loop(0, n_chunks, body, jnp.zeros((CNT_ROWS, tq), F32))
        return jnp.sum(cnt, axis=0, keepdims=True) + jnp.where(cand_f <= NEG_INF, n_skip, 0.0)

    def bit_body(it, carry):
        prefix, cge, cgt = carry
        cand = prefix + lax.shift_left(jnp.int32(1), 31 - it)
        cnt = count_ge(_key_to_float(cand))
        ok = cnt >= kf
        return jnp.where(ok, cand, prefix), jnp.where(ok, cnt, cge), jnp.where(ok, cgt, cnt)

    thr_key, cge, cgt = lax.fori_loop(
        0, 32, bit_body,
        (jnp.full((1, tq), -2 ** 31, I32), jnp.full((1, tq), float(s), F32),
         jnp.zeros((1, tq), F32)))
    thr = _key_to_float(thr_key)

    def count_keys(indicator):
        def body(c, cnt):
            off = pl.multiple_of(c * kc, kc)
            return cnt + fold_rows(indicator(score_scr[pl.ds(off, kc), :], off), CNT_ROWS)
        cnt = lax.fori_loop(0, n_chunks, body, jnp.zeros((CNT_ROWS, tq), F32))
        return jnp.sum(cnt, axis=0, keepdims=True)

    has_ties = jnp.max(cge) > kf

    @pl.when(jnp.logical_not(has_ties))
    def _():
        def body(c, carry):
            off = pl.multiple_of(c * kc, kc)
            kk = score_scr[pl.ds(off, kc), :]
            causal_bias = jnp.where(off + key_i <= qry_t, 0.0, NEG_INF)
            bias_scr[pl.ds(off, kc), :] = jnp.where(kk >= thr, causal_bias, NEG_INF)
            return carry
        lax.fori_loop(0, n_chunks, body, 0)

    @pl.when(has_ties)
    def _():
        thr_next = _key_to_float(thr_key + 1)
        need = kf - cgt
        n_bits = int(s).bit_length()

        def band_count(indicator):
            return count_keys(lambda kk, off: jnp.where(
                kk >= thr, jnp.where(kk >= thr_next, 0.0, indicator(kk - thr, off)), 0.0))

        def res_body(it, carry):
            prefix, cgt2 = carry
            cand = prefix + lax.shift_left(jnp.int32(1), 30 - it)
            cand_f = lax.bitcast_convert_type(cand, F32)
            cnt = band_count(lambda d, off: jnp.where(d >= cand_f, 1.0, 0.0))
            ok = cnt >= need
            return jnp.where(ok, cand, prefix), jnp.where(ok, cgt2, cnt)

        d_key, cgt2 = lax.fori_loop(0, 31, res_body,
                                    (jnp.zeros((1, tq), I32), jnp.zeros((1, tq), F32)))
        d_thr = lax.bitcast_convert_type(d_key, F32)
        need_eq = need - cgt2

        def tie_body(it, jm):
            cand = jm + lax.shift_left(jnp.int32(1), (n_bits - 1) - it)
            g = band_count(lambda d, off: jnp.where(
                d == d_thr, jnp.where(off + key_i < cand, 1.0, 0.0), 0.0))
            return jnp.where(g < need_eq, cand, jm)

        jm = lax.fori_loop(0, n_bits, tie_body, jnp.zeros((1, tq), I32))

        def body(c, carry):
            off = pl.multiple_of(c * kc, kc)
            kk = score_scr[pl.ds(off, kc), :]
            idx = off + key_i
            d = kk - thr
            causal_bias = jnp.where(idx <= qry_t, 0.0, NEG_INF)
            tie_bias = jnp.where(idx <= jm, causal_bias, NEG_INF)
            band_bias = jnp.where(d > d_thr, causal_bias, jnp.where(d == d_thr, tie_bias, NEG_INF))
            bias_scr[pl.ds(off, kc), :] = jnp.where(
                kk >= thr_next, causal_bias, jnp.where(kk >= thr, band_bias, NEG_INF))
            return carry
        lax.fori_loop(0, n_chunks, body, 0)

    q = q_ref[0]
    q_h = [q[:, hh * HEAD_DIM:(hh + 1) * HEAD_DIM] for hh in range(N_HEADS)]

    def masked_logits(c, hh):
        off = pl.multiple_of(c * kc, kc)
        k_c = kv_ref[0, pl.ds(off, kc), 0:HEAD_DIM]
        return _dot_nt(k_c, q_h[hh]) + bias_scr[pl.ds(off, kc), :]

    def attend(m_all):
        acc_scr[...] = jnp.zeros((VT_ROWS, N_HEADS * tq), F32)
        for g in range(N_HEADS // PV_HEADS):
            def exp_chunk(c, g=g):
                off = pl.multiple_of(c * kc, kc)
                for hl in range(PV_HEADS):
                    hh = g * PV_HEADS + hl
                    p = jnp.exp2(masked_logits(c, hh) - m_all[hh:hh + 1, :])
                    p_scr[pl.ds(off, kc), hl * tq:(hl + 1) * tq] = p.astype(BF16)

            for_chunks(exp_chunk)

            def pv_chunk(c, g=g):
                off = pl.multiple_of(c * kc, kc)
                cols = slice(g * PV_HEADS * tq, (g + 1) * PV_HEADS * tq)
                acc_scr[:, cols] += _dot(vt_ref[0, :, pl.ds(off, kc)], p_scr[pl.ds(off, kc), :])

            for_chunks(pv_chunk)

    @pl.when(j == 0)
    def _():
        kf32 = kv_ref[0][:, 0:HEAD_DIM].astype(F32)
        kmax_scr[...] = jnp.full((1, LANES), jnp.max(jnp.sum(kf32 * kf32, axis=-1, keepdims=True)), F32)

    qf32 = q.astype(F32)
    qsq_hi, qsq_lo = _split_bf16(qf32 * qf32)
    qn2 = _dot_nt(hsum_ref[...], qsq_hi) + _dot_nt(hsum_ref[...], qsq_lo)
    attend(jnp.sqrt(qn2 * kmax_scr[0:1, 0:1]) * BOUND_SLACK)

    @pl.when(jnp.min(acc_scr[HEAD_DIM:HEAD_DIM + 1, :]) < MIN_DENOM)
    def _():
        def max_body(c, mparts):
            return tuple(
                jnp.maximum(mparts[hh],
                            jnp.max(masked_logits(c, hh).reshape(kc // 8, 8, tq), axis=0))
                for hh in range(N_HEADS))

        mparts = lax.fori_loop(0, n_chunks, max_body,
                               tuple(jnp.full((8, tq), -3e38, F32) for _ in range(N_HEADS)))
        attend(jnp.concatenate([jnp.max(mp, axis=0, keepdims=True) for mp in mparts], axis=0))

    outs = []
    for hh in range(N_HEADS):
        o = (acc_scr[0:HEAD_DIM, hh * tq:(hh + 1) * tq]
             / acc_scr[HEAD_DIM:HEAD_DIM + 1, hh * tq:(hh + 1) * tq])
        ms = jnp.mean(o * o, axis=0, keepdims=True)
        outs.append(o * lax.rsqrt(ms + EPS))
    out_t = jnp.concatenate(outs, axis=0)
    o_ref[0] = (jnp.transpose(out_t) * g_ref[...]).astype(BF16)


def _attention(q, qi, wt, kv, ki, vt, attn_g, head_sum):
    bsz, s, _ = q.shape
    tq = min(TQ, s)
    kc = min(KC, s)
    k_sel = min(TOPK_MAX, s // 4)
    grid = (bsz, s // tq)
    qtile = lambda w: pl.BlockSpec((1, tq, w), lambda b, j: (b, j, 0))
    seq = lambda w: pl.BlockSpec((1, s, w), lambda b, j: (b, 0, 0))
    return pl.pallas_call(
        functools.partial(_attn_kernel, s=s, k_sel=k_sel, tq=tq, kc=kc),
        grid=grid,
        in_specs=[qtile(ATTN_WIDTH), qtile(IDX_HEADS * IDX_DIM),
                  pl.BlockSpec((1, IDX_HEADS, tq), lambda b, j: (b, 0, j)),
                  seq(2 * HEAD_DIM), seq(LANES),
                  pl.BlockSpec((1, VT_ROWS, s), lambda b, j: (b, 0, 0)),
                  pl.BlockSpec((1, ATTN_WIDTH), lambda b, j: (0, 0)),
                  pl.BlockSpec((N_HEADS, ATTN_WIDTH), lambda b, j: (0, 0))],
        out_specs=qtile(ATTN_WIDTH),
        out_shape=jax.ShapeDtypeStruct((bsz, s, ATTN_WIDTH), BF16),
        scratch_shapes=[pltpu.VMEM((s, tq), F32), pltpu.VMEM((s, tq), F32),
                        pltpu.VMEM((s, PV_HEADS * tq), BF16),
                        pltpu.VMEM((VT_ROWS, N_HEADS * tq), F32),
                        pltpu.VMEM((1, LANES), F32)],
        compiler_params=pltpu.CompilerParams(
            dimension_semantics=("arbitrary", "arbitrary"), vmem_limit_bytes=VMEM_LIMIT),
        name="attn",
    )(q, qi, wt, kv, ki, vt, attn_g, head_sum)


def _layer_norm(y, g, b):
    mu = jnp.mean(y, axis=-1, keepdims=True)
    yc = y - mu
    var = jnp.mean(yc * yc, axis=-1, keepdims=True)
    return yc * lax.rsqrt(var + EPS) * g + b


def _post_kernel(x_ref, at_ref, cv_ref, g1_ref, sh2_ref, sc2_ref, g2_ref,
                 wo_ref, l1g_ref, l1b_ref, wg_ref, wu_ref, wd_ref, l2g_ref, l2b_ref, o_ref):
    mix = _dot(at_ref[0], wo_ref[0:ATTN_WIDTH, :]) + _dot(cv_ref[0], wo_ref[ATTN_WIDTH:D_MODEL, :])
    x1 = _layer_norm(DEEPNORM_ALPHA * x_ref[0] + g1_ref[0] * mix, l1g_ref[...], l1b_ref[...])
    h2 = (x1 * (1.0 + sc2_ref[0]) + sh2_ref[0]).astype(BF16)
    gt = _dot(h2, wg_ref[...])
    up = _dot(h2, wu_ref[...])
    hid = (gt * (1.0 / (1.0 + jnp.exp(-gt))) * up).astype(BF16)
    ff = _dot(hid, wd_ref[...])
    o_ref[0] = _layer_norm(DEEPNORM_ALPHA * x1 + g2_ref[0] * ff, l2g_ref[...], l2b_ref[...])


def _post(x, attn, conv, g1, sh2, sc2, g2, w_out, l1g, l1b, w_gate, w_up, w_down, l2g, l2b):
    bsz, s, d = x.shape
    tm = min(TM_PROJ, s)
    grid = (bsz, s // tm)
    tok = lambda w: pl.BlockSpec((1, tm, w), lambda b, i: (b, i, 0))
    vec = pl.BlockSpec((1, 1, d), lambda b, i: (b, 0, 0))
    full = lambda a: pl.BlockSpec(a.shape, lambda b, i: (0,) * a.ndim,
                                  pipeline_mode=pl.Buffered(1))
    return pl.pallas_call(
        _post_kernel,
        grid=grid,
        in_specs=[tok(d), tok(ATTN_WIDTH), tok(CONV_WIDTH), vec, vec, vec, vec,
                  full(w_out), full(l1g), full(l1b), full(w_gate), full(w_up), full(w_down),
                  full(l2g), full(l2b)],
        out_specs=tok(d),
        out_shape=jax.ShapeDtypeStruct((bsz, s, d), F32),
        compiler_params=pltpu.CompilerParams(
            dimension_semantics=("arbitrary", "arbitrary"), vmem_limit_bytes=VMEM_LIMIT),
        name="post",
    )(x, attn, conv, g1, sh2, sc2, g2, w_out, l1g, l1b, w_gate, w_up, w_down, l2g, l2b)


def _pack_weights(w_in):
    n_front = OFF_WI + IDX_HEADS
    front = jnp.pad(w_in[:, :n_front], ((0, 0), (0, FRONT - n_front)))
    return jnp.concatenate([front, w_in[:, n_front:]], axis=1).astype(BF16)


def _rope_tables(s):
    pos = jnp.arange(s, dtype=F32)
    inv = ROPE_THETA ** (-(jnp.arange(0, ROPE_DIM, 2, dtype=F32) / ROPE_DIM))
    ang = pos[:, None] * inv[None, :]
    cos, sin = jnp.cos(ang), jnp.sin(ang)
    half = ROPE_DIM // 2

    def region(n_heads, dim, rope, scale):
        if rope:
            own = jnp.concatenate([cos, cos, jnp.ones((s, dim - ROPE_DIM), F32)], axis=1)
            up = jnp.concatenate([-sin, jnp.zeros((s, dim - half), F32)], axis=1)
            dn = jnp.concatenate([jnp.zeros((s, half), F32), sin,
                                  jnp.zeros((s, dim - ROPE_DIM), F32)], axis=1)
        else:
            own = jnp.ones((s, dim), F32)
            up = dn = jnp.zeros((s, dim), F32)
        return tuple(jnp.tile(t, (1, n_heads)) * scale for t in (own, up, dn))

    regions = [
        region(LANES // HEAD_DIM, HEAD_DIM, True, HEAD_DIM ** -0.5 * LOG2E),
        region(1, HEAD_DIM, True, 1.0),
        region(1, HEAD_DIM, False, 1.0),
        region(LANES // IDX_DIM, IDX_DIM, True, 1.0),
        region(1, IDX_DIM, True, 1.0),
        region(1, IDX_HEADS, False, IDX_HEADS ** -0.5 * IDX_DIM ** -0.5),
        region(1, FRONT - OFF_WI - IDX_HEADS, False, 1.0),
    ]
    return tuple(jnp.concatenate([r[t] for r in regions], axis=1) for t in range(3))


def kernel(x, c, w_ada, b_ada, w_in, conv_w, attn_norm_g, conv_norm_g, w_out, ln1_g, ln1_b,
           w_gate, w_up, w_down, ln2_g, ln2_b):
    bsz, s, d = x.shape
    ctab, sup, sdn = _rope_tables(s)
    grp = CONV_WIDTH // CONV_GROUPS
    gmat = jnp.asarray(np.kron(np.eye(CONV_GROUPS), np.ones((grp, grp))), BF16)
    for l in range(DEPTH):
        mod = _ada(c, w_ada[l], b_ada[l])
        sh1, sc1, g1, sh2, sc2, g2 = [m.reshape(bsz, 1, d) for m in jnp.split(mod, 6, axis=-1)]
        q, kv, qi, ki, vt, wt, conv = _inproj(
            x, sh1, sc1, _pack_weights(w_in[l]), ctab, sup, sdn, conv_w[l],
            conv_norm_g[l].reshape(1, -1), gmat)
        head_sum = jnp.asarray(np.kron(np.eye(N_HEADS), np.ones((1, HEAD_DIM))), BF16)
        attn = _attention(q, qi, wt, kv, ki, vt, attn_norm_g[l].reshape(1, -1), head_sum)
        x = _post(x, attn, conv, g1, sh2, sc2, g2,
                  w_out[l].astype(BF16), ln1_g[l].reshape(1, -1), ln1_b[l].reshape(1, -1),
                  w_gate[l].astype(BF16), w_up[l].astype(BF16), w_down[l].astype(BF16),
                  ln2_g[l].reshape(1, -1), ln2_b[l].reshape(1, -1))
    return x
```

```python
import functools

import numpy as np
import jax
import jax.numpy as jnp
from jax import lax
from jax.experimental import pallas as pl
from jax.experimental.pallas import tpu as pltpu

D_MODEL = 1024
N_HEADS = 8
HEAD_DIM = 64
ATTN_WIDTH = N_HEADS * HEAD_DIM
IDX_HEADS = 8
IDX_DIM = 32
TOPK_MAX = 256
CONV_WIDTH = D_MODEL - ATTN_WIDTH
CONV_GROUPS = 8
CONV_K = 3
ROPE_THETA = 500000.0
ROPE_DIM = HEAD_DIM // 4
D_FF = 2816
DEPTH = 1
DEEPNORM_ALPHA = (2.0 * DEPTH) ** 0.25
EPS = 1e-5
NEG_INF = -1e30

F32 = jnp.float32
BF16 = jnp.bfloat16
I32 = jnp.int32

FRONT = 1024
OFF_Q, OFF_K, OFF_V, OFF_QI, OFF_KI, OFF_WI = 0, 512, 576, 640, 896, 928
N_COLS = FRONT + 3 * CONV_WIDTH
ROPE_PATTERN = (0, 0, 0, 0, 1, 2, 2, 3)

TM_PROJ = 512
TQ = 512
PV_HEADS = 4
KC = 512
CNT_SLAB = 64
CNT_ROWS = 32
VT_ROWS = 80
LOG2E = 1.4426950408889634
BOUND_SLACK = 1.002
MIN_DENOM = 2.0 ** -60
LANES = 128
VMEM_LIMIT = 56 * 1024 * 1024


def _key_to_float(key):
    bits = key ^ (lax.shift_right_arithmetic(key, 31) & jnp.int32(0x7FFFFFFF))
    return lax.bitcast_convert_type(bits, F32)


def _dot(a, b):
    return jnp.dot(a, b, preferred_element_type=F32)


def _dot_nt(a, b):
    return lax.dot_general(a, b, (((1,), (1,)), ((), ())), preferred_element_type=F32)


def _split_bf16(a):
    hi = a.astype(BF16)
    lo = (a - hi.astype(F32)).astype(BF16)
    return hi, lo


def _ada_kernel(c_ref, w_ref, b_ref, o_ref):
    c = c_ref[...]
    ca = c * (1.0 / (1.0 + jnp.exp(-c)))
    ca_hi, ca_lo = _split_bf16(ca)
    w_hi, w_lo = _split_bf16(w_ref[...])
    acc = _dot(ca_hi, w_hi) + _dot(ca_hi, w_lo) + _dot(ca_lo, w_hi)
    o_ref[...] = acc + b_ref[...]


def _ada(c, w_ada, b_ada):
    bsz, d = c.shape
    n = w_ada.shape[1]
    blk = 1024
    return pl.pallas_call(
        _ada_kernel,
        grid=(n // blk,),
        in_specs=[
            pl.BlockSpec((bsz, d), lambda j: (0, 0)),
            pl.BlockSpec((d, blk), lambda j: (0, j)),
            pl.BlockSpec((1, blk), lambda j: (0, j)),
        ],
        out_specs=pl.BlockSpec((bsz, blk), lambda j: (0, j)),
        out_shape=jax.ShapeDtypeStruct((bsz, n), F32),
        name="ada",
    )(c, w_ada, b_ada.reshape(1, n))


def _inproj_kernel(x_ref, sh_ref, sc_ref, w_ref, ct_ref, su_ref, sd_ref, cw_ref, cg_ref, gm_ref,
                   q_ref, kv_ref, qi_ref, ki_ref, vt_ref, wt_ref, cv_ref, ubuf, halo, *, tm):
    i = pl.program_id(0)
    b = pl.program_id(1)
    h = (x_ref[0] * (1.0 + sc_ref[0]) + sh_ref[0]).astype(BF16)

    front = _dot(h, w_ref[:, 0:FRONT])
    blocks = []
    for blk in range(FRONT // LANES):
        lanes = slice(blk * LANES, (blk + 1) * LANES)
        f = front[:, lanes]
        up = pltpu.roll(f, LANES - ROPE_DIM // 2, axis=1)
        dn = pltpu.roll(f, ROPE_DIM // 2, axis=1)
        pat = slice(ROPE_PATTERN[blk] * LANES, (ROPE_PATTERN[blk] + 1) * LANES)
        blocks.append(f * ct_ref[:, pat] + up * su_ref[:, pat] + dn * sd_ref[:, pat])
    roped = jnp.concatenate(blocks, axis=1)
    q_ref[0] = roped[:, OFF_Q:OFF_K].astype(BF16)
    kv = roped[:, OFF_K:OFF_QI]
    kv_ref[0] = kv.astype(BF16)
    qi_ref[0] = roped[:, OFF_QI:OFF_KI].astype(BF16)
    kw = roped[:, OFF_KI:FRONT]
    ki_ref[0] = kw.astype(BF16)
    vt_ref[0, 0:HEAD_DIM, :] = jnp.transpose(kv)[HEAD_DIM:2 * HEAD_DIM, :].astype(BF16)
    vt_ref[0, HEAD_DIM:VT_ROWS, :] = jnp.ones((VT_ROWS - HEAD_DIM, tm), BF16)
    wt_ref[0] = jnp.transpose(kw)[OFF_WI - OFF_KI:OFF_WI - OFF_KI + IDX_HEADS, :]

    cp = _dot(h, w_ref[:, FRONT:N_COLS])
    gate_b = cp[:, 0:CONV_WIDTH]
    u = cp[:, CONV_WIDTH:2 * CONV_WIDTH] * cp[:, 2 * CONV_WIDTH:3 * CONV_WIDTH]

    @pl.when(i == 0)
    def _():
        ubuf[0:8, :] = jnp.zeros((8, CONV_WIDTH), F32)

    @pl.when(i > 0)
    def _():
        ubuf[0:8, :] = halo[b]

    ubuf[8:8 + tm, :] = u
    conv = (cw_ref[0:1, :] * ubuf[6:6 + tm, :] + cw_ref[1:2, :] * ubuf[7:7 + tm, :]
            + cw_ref[2:3, :] * u)
    y = gate_b * conv
    y2_hi, y2_lo = _split_bf16(y * y)
    ssq = _dot(y2_hi, gm_ref[...]) + _dot(y2_lo, gm_ref[...])
    yn = y * lax.rsqrt(ssq * (1.0 / (CONV_WIDTH // CONV_GROUPS)) + EPS) * cg_ref[...]
    cv_ref[0] = yn.astype(BF16)
    halo[b] = ubuf[tm:tm + 8, :]


def _inproj(x, sh1, sc1, w_all, ctab, sup, sdn, conv_w, conv_g, gmat):
    bsz, s, d = x.shape
    tm = min(TM_PROJ, s)
    grid = (s // tm, bsz)
    tok = lambda w: pl.BlockSpec((1, tm, w), lambda i, b: (b, i, 0))
    tok_t = lambda r: pl.BlockSpec((1, r, tm), lambda i, b: (b, 0, i))
    vec = pl.BlockSpec((1, 1, d), lambda i, b: (b, 0, 0))
    full = lambda a: pl.BlockSpec(a.shape, lambda i, b: (0,) * a.ndim)
    table = pl.BlockSpec((tm, ctab.shape[1]), lambda i, b: (i, 0))
    return pl.pallas_call(
        functools.partial(_inproj_kernel, tm=tm),
        grid=grid,
        in_specs=[tok(d), vec, vec, full(w_all), table, table, table,
                  full(conv_w), full(conv_g), full(gmat)],
        out_specs=[tok(ATTN_WIDTH), tok(2 * HEAD_DIM), tok(IDX_HEADS * IDX_DIM),
                   tok(LANES), tok_t(VT_ROWS), tok_t(IDX_HEADS), tok(CONV_WIDTH)],
        out_shape=[
            jax.ShapeDtypeStruct((bsz, s, ATTN_WIDTH), BF16),
            jax.ShapeDtypeStruct((bsz, s, 2 * HEAD_DIM), BF16),
            jax.ShapeDtypeStruct((bsz, s, IDX_HEADS * IDX_DIM), BF16),
            jax.ShapeDtypeStruct((bsz, s, LANES), BF16),
            jax.ShapeDtypeStruct((bsz, VT_ROWS, s), BF16),
            jax.ShapeDtypeStruct((bsz, IDX_HEADS, s), F32),
            jax.ShapeDtypeStruct((bsz, s, CONV_WIDTH), BF16),
        ],
        scratch_shapes=[pltpu.VMEM((tm + 8, CONV_WIDTH), F32),
                        pltpu.VMEM((bsz, 8, CONV_WIDTH), F32)],
        compiler_params=pltpu.CompilerParams(
            dimension_semantics=("arbitrary", "arbitrary"), vmem_limit_bytes=VMEM_LIMIT),
        name="inproj",
    )(x, sh1, sc1, w_all, ctab, sup, sdn, conv_w, conv_g, gmat)


def _attn_kernel(q_ref, qi_ref, wt_ref, kv_ref, ki_ref, vt_ref, g_ref, hsum_ref, o_ref,
                 score_scr, bias_scr, p_scr, acc_scr, kmax_scr, tie_state,
                 *, s, k_sel, tq, kc):
    j = pl.program_id(1)
    t0 = j * tq
    n_chunks = (t0 + tq + kc - 1) // kc
    n_skip = (s - n_chunks * kc).astype(F32)
    kf = float(k_sel)

    key_i = lax.broadcasted_iota(I32, (kc, tq), 0)
    qry_t = t0 + lax.broadcasted_iota(I32, (kc, tq), 1)

    def for_chunks(body):
        def pair(i, carry):
            body(2 * i)
            body(2 * i + 1)
            return carry

        lax.fori_loop(0, n_chunks // 2, pair, 0)

        @pl.when(n_chunks % 2 == 1)
        def _():
            body(n_chunks - 1)

    def fold_rows(a, rows):
        return jnp.sum(a.reshape(kc // rows, rows, tq), axis=0)

    qi = qi_ref[0]
    qi_h = [qi[:, hh * IDX_DIM:(hh + 1) * IDX_DIM] for hh in range(IDX_HEADS)]
    wt = wt_ref[0]
    wt_h = [wt[hh:hh + 1, :] for hh in range(IDX_HEADS)]

    def score_chunk(c):
        off = pl.multiple_of(c * kc, kc)
        ki_c = ki_ref[0, pl.ds(off, kc), 0:IDX_DIM]
        acc = jnp.zeros((kc, tq), F32)
        for hh in range(IDX_HEADS):
            acc = acc + jnp.maximum(_dot_nt(ki_c, qi_h[hh]), 0.0) * wt_h[hh]
        score_scr[pl.ds(off, kc), :] = jnp.where(off + key_i <= qry_t, acc, NEG_INF)

    for_chunks(score_chunk)

    def count_ge(cand_f):
        def body(c, cnt):
            off = pl.multiple_of(c * kc, kc)
            for r0 in range(0, kc, CNT_SLAB):
                ind = jnp.where(score_scr[pl.ds(off + r0, CNT_SLAB), :] >= cand_f, 1.0, 0.0)
                parts = [ind[r:r + CNT_ROWS, :] for r in range(0, CNT_SLAB, CNT_ROWS)]
                while len(parts) > 1:
                    parts = [parts[i] + parts[i + 1] for i in range(0, len(parts), 2)]
                cnt = cnt + parts[0]
            return cnt

        cnt = lax.fori_loop(0, n_chunks, body, jnp.zeros((CNT_ROWS, tq), F32))
        return jnp.sum(cnt, axis=0, keepdims=True) + jnp.where(cand_f <= NEG_INF, n_skip, 0.0)

    def bit_body(it, carry):
        prefix, cge, cgt = carry
        cand = prefix + lax.shift_left(jnp.int32(1), 31 - it)
        cnt = count_ge(_key_to_float(cand))
        ok = cnt >= kf
        return jnp.where(ok, cand, prefix), jnp.where(ok, cnt, cge), jnp.where(ok, cgt, cnt)

    thr_key, cge, cgt = lax.fori_loop(
        0, 32, bit_body,
        (jnp.full((1, tq), -2 ** 31, I32), jnp.full((1, tq), float(s), F32),
         jnp.zeros((1, tq), F32)))
    thr = _key_to_float(thr_key)

    def count_keys(indicator):
        def body(c, cnt):
            off = pl.multiple_of(c * kc, kc)
            return cnt + fold_rows(indicator(score_scr[pl.ds(off, kc), :], off), CNT_ROWS)
        cnt = lax.fori_loop(0, n_chunks, body, jnp.zeros((CNT_ROWS, tq), F32))
        return jnp.sum(cnt, axis=0, keepdims=True)

    has_ties = jnp.max(cge) > kf

    @pl.when(jnp.logical_not(has_ties))
    def _():
        def body(c, carry):
            off = pl.multiple_of(c * kc, kc)
            kk = score_scr[pl.ds(off, kc), :]
            causal_bias = jnp.where(off + key_i <= qry_t, 0.0, NEG_INF)
            bias_scr[pl.ds(off, kc), :] = jnp.where(kk >= thr, causal_bias, NEG_INF)
            return carry
        lax.fori_loop(0, n_chunks, body, 0)

    @pl.when(has_ties)
    def _():
        thr_next = _key_to_float(thr_key + 1)
        need = kf - cgt
        n_bits = int(s).bit_length()

        def band_count(indicator):
            return count_keys(lambda kk, off: jnp.where(
                kk >= thr, jnp.where(kk >= thr_next, 0.0, indicator(kk - thr, off)), 0.0))

        tie_state[...] = jnp.zeros((8, tq), F32)

        @pl.when(jnp.max(band_count(lambda d, off: jnp.where(d > 0.0, 1.0, 0.0))) > 0.0)
        def _():
            def res_body(it, carry):
                prefix, cgt2 = carry
                cand = prefix + lax.shift_left(jnp.int32(1), 30 - it)
                cand_f = lax.bitcast_convert_type(cand, F32)
                cnt = band_count(lambda d, off: jnp.where(d >= cand_f, 1.0, 0.0))
                ok = cnt >= need
                return jnp.where(ok, cand, prefix), jnp.where(ok, cgt2, cnt)

            d_key, cgt2 = lax.fori_loop(0, 31, res_body,
                                        (jnp.zeros((1, tq), I32), jnp.zeros((1, tq), F32)))
            tie_state[0:1, :] = lax.bitcast_convert_type(d_key, F32)
            tie_state[1:2, :] = cgt2

        d_thr = tie_state[0:1, :]
        need_eq = need - tie_state[1:2, :]

        def mark(c, carry):
            off = pl.multiple_of(c * kc, kc)
            kk = score_scr[pl.ds(off, kc), :]
            bias_scr[pl.ds(off, kc), :] = jnp.where(
                kk >= thr,
                jnp.where(kk >= thr_next, 0.0, jnp.where(kk - thr == d_thr, 1.0, 0.0)), 0.0)
            return carry

        lax.fori_loop(0, n_chunks, mark, 0)
        row_f = lax.broadcasted_iota(I32, (CNT_SLAB, tq), 0).astype(F32)

        def marked_below(cand_f):
            def body(c, cnt):
                off = pl.multiple_of(c * kc, kc)
                for r0 in range(0, kc, CNT_SLAB):
                    rel = cand_f - (off + r0).astype(F32)
                    ind = jnp.where(row_f < rel, bias_scr[pl.ds(off + r0, CNT_SLAB), :], 0.0)
                    parts = [ind[r:r + CNT_ROWS, :] for r in range(0, CNT_SLAB, CNT_ROWS)]
                    while len(parts) > 1:
                        parts = [parts[i] + parts[i + 1] for i in range(0, len(parts), 2)]
                    cnt = cnt + parts[0]
                return cnt

            cnt = lax.fori_loop(0, n_chunks, body, jnp.zeros((CNT_ROWS, tq), F32))
            return jnp.sum(cnt, axis=0, keepdims=True)

        def tie_body(it, jm):
            cand = jm + lax.shift_left(jnp.int32(1), (n_bits - 1) - it)
            return jnp.where(marked_below(cand.astype(F32)) < need_eq, cand, jm)

        jm = lax.fori_loop(0, n_bits, tie_body, jnp.zeros((1, tq), I32))

        def body(c, carry):
            off = pl.multiple_of(c * kc, kc)
            kk = score_scr[pl.ds(off, kc), :]
            idx = off + key_i
            d = kk - thr
            causal_bias = jnp.where(idx <= qry_t, 0.0, NEG_INF)
            tie_bias = jnp.where(idx <= jm, causal_bias, NEG_INF)
            band_bias = jnp.where(d > d_thr, causal_bias, jnp.where(d == d_thr, tie_bias, NEG_INF))
            bias_scr[pl.ds(off, kc), :] = jnp.where(
                kk >= thr_next, causal_bias, jnp.where(kk >= thr, band_bias, NEG_INF))
            return carry
        lax.fori_loop(0, n_chunks, body, 0)

    q = q_ref[0]
    q_h = [q[:, hh * HEAD_DIM:(hh + 1) * HEAD_DIM] for hh in range(N_HEADS)]

    def masked_logits(c, hh):
        off = pl.multiple_of(c * kc, kc)
        k_c = kv_ref[0, pl.ds(off, kc), 0:HEAD_DIM]
        return _dot_nt(k_c, q_h[hh]) + bias_scr[pl.ds(off, kc), :]

    def attend(m_all):
        acc_scr[...] = jnp.zeros((VT_ROWS, N_HEADS * tq), F32)
        for g in range(N_HEADS // PV_HEADS):
            def exp_chunk(c, g=g):
                off = pl.multiple_of(c * kc, kc)
                for hl in range(PV_HEADS):
                    hh = g * PV_HEADS + hl
                    p = jnp.exp2(masked_logits(c, hh) - m_all[hh:hh + 1, :])
                    p_scr[pl.ds(off, kc), hl * tq:(hl + 1) * tq] = p.astype(BF16)

            for_chunks(exp_chunk)

            def pv_chunk(c, g=g):
                off = pl.multiple_of(c * kc, kc)
                cols = slice(g * PV_HEADS * tq, (g + 1) * PV_HEADS * tq)
                acc_scr[:, cols] += _dot(vt_ref[0, :, pl.ds(off, kc)], p_scr[pl.ds(off, kc), :])

            for_chunks(pv_chunk)

    @pl.when(j == 0)
    def _():
        kf32 = kv_ref[0][:, 0:HEAD_DIM].astype(F32)
        kmax_scr[...] = jnp.full((1, LANES), jnp.max(jnp.sum(kf32 * kf32, axis=-1, keepdims=True)), F32)

    qf32 = q.astype(F32)
    qsq_hi, qsq_lo = _split_bf16(qf32 * qf32)
    qn2 = _dot_nt(hsum_ref[...], qsq_hi) + _dot_nt(hsum_ref[...], qsq_lo)
    attend(jnp.sqrt(qn2 * kmax_scr[0:1, 0:1]) * BOUND_SLACK)

    @pl.when(jnp.min(acc_scr[HEAD_DIM:HEAD_DIM + 1, :]) < MIN_DENOM)
    def _():
        def max_body(c, mparts):
            return tuple(
                jnp.maximum(mparts[hh],
                            jnp.max(masked_logits(c, hh).reshape(kc // 8, 8, tq), axis=0))
                for hh in range(N_HEADS))

        mparts = lax.fori_loop(0, n_chunks, max_body,
                               tuple(jnp.full((8, tq), -3e38, F32) for _ in range(N_HEADS)))
        attend(jnp.concatenate([jnp.max(mp, axis=0, keepdims=True) for mp in mparts], axis=0))

    outs = []
    for hh in range(N_HEADS):
        o = (acc_scr[0:HEAD_DIM, hh * tq:(hh + 1) * tq]
             / acc_scr[HEAD_DIM:HEAD_DIM + 1, hh * tq:(hh + 1) * tq])
        ms = jnp.mean(o * o, axis=0, keepdims=True)
        outs.append(o * lax.rsqrt(ms + EPS))
    out_t = jnp.concatenate(outs, axis=0)
    o_ref[0] = (jnp.transpose(out_t) * g_ref[...]).astype(BF16)


def _attention(q, qi, wt, kv, ki, vt, attn_g, head_sum):
    bsz, s, _ = q.shape
    tq = min(TQ, s)
    kc = min(KC, s)
    k_sel = min(TOPK_MAX, s // 4)
    grid = (bsz, s // tq)
    qtile = lambda w: pl.BlockSpec((1, tq, w), lambda b, j: (b, j, 0))
    seq = lambda w: pl.BlockSpec((1, s, w), lambda b, j: (b, 0, 0))
    return pl.pallas_call(
        functools.partial(_attn_kernel, s=s, k_sel=k_sel, tq=tq, kc=kc),
        grid=grid,
        in_specs=[qtile(ATTN_WIDTH), qtile(IDX_HEADS * IDX_DIM),
                  pl.BlockSpec((1, IDX_HEADS, tq), lambda b, j: (b, 0, j)),
                  seq(2 * HEAD_DIM), seq(LANES),
                  pl.BlockSpec((1, VT_ROWS, s), lambda b, j: (b, 0, 0)),
                  pl.BlockSpec((1, ATTN_WIDTH), lambda b, j: (0, 0)),
                  pl.BlockSpec((N_HEADS, ATTN_WIDTH), lambda b, j: (0, 0))],
        out_specs=qtile(ATTN_WIDTH),
        out_shape=jax.ShapeDtypeStruct((bsz, s, ATTN_WIDTH), BF16),
        scratch_shapes=[pltpu.VMEM((s, tq), F32), pltpu.VMEM((s, tq), F32),
                        pltpu.VMEM((s, PV_HEADS * tq), BF16),
                        pltpu.VMEM((VT_ROWS, N_HEADS * tq), F32),
                        pltpu.VMEM((1, LANES), F32), pltpu.VMEM((8, tq), F32)],
        compiler_params=pltpu.CompilerParams(
            dimension_semantics=("arbitrary", "arbitrary"), vmem_limit_bytes=VMEM_LIMIT),
        name="attn",
    )(q, qi, wt, kv, ki, vt, attn_g, head_sum)


def _layer_norm(y, g, b):
    mu = jnp.mean(y, axis=-1, keepdims=True)
    yc = y - mu
    var = jnp.mean(yc * yc, axis=-1, keepdims=True)
    return yc * lax.rsqrt(var + EPS) * g + b


def _post_kernel(x_ref, at_ref, cv_ref, g1_ref, sh2_ref, sc2_ref, g2_ref,
                 wo_ref, l1g_ref, l1b_ref, wg_ref, wu_ref, wd_ref, l2g_ref, l2b_ref, o_ref):
    mix = _dot(at_ref[0], wo_ref[0:ATTN_WIDTH, :]) + _dot(cv_ref[0], wo_ref[ATTN_WIDTH:D_MODEL, :])
    x1 = _layer_norm(DEEPNORM_ALPHA * x_ref[0] + g1_ref[0] * mix, l1g_ref[...], l1b_ref[...])
    h2 = (x1 * (1.0 + sc2_ref[0]) + sh2_ref[0]).astype(BF16)
    gt = _dot(h2, wg_ref[...])
    up = _dot(h2, wu_ref[...])
    hid = (gt * (1.0 / (1.0 + jnp.exp(-gt))) * up).astype(BF16)
    ff = _dot(hid, wd_ref[...])
    o_ref[0] = _layer_norm(DEEPNORM_ALPHA * x1 + g2_ref[0] * ff, l2g_ref[...], l2b_ref[...])


def _post(x, attn, conv, g1, sh2, sc2, g2, w_out, l1g, l1b, w_gate, w_up, w_down, l2g, l2b):
    bsz, s, d = x.shape
    tm = min(TM_PROJ, s)
    grid = (bsz, s // tm)
    tok = lambda w: pl.BlockSpec((1, tm, w), lambda b, i: (b, i, 0))
    vec = pl.BlockSpec((1, 1, d), lambda b, i: (b, 0, 0))
    full = lambda a: pl.BlockSpec(a.shape, lambda b, i: (0,) * a.ndim,
                                  pipeline_mode=pl.Buffered(1))
    return pl.pallas_call(
        _post_kernel,
        grid=grid,
        in_specs=[tok(d), tok(ATTN_WIDTH), tok(CONV_WIDTH), vec, vec, vec, vec,
                  full(w_out), full(l1g), full(l1b), full(w_gate), full(w_up), full(w_down),
                  full(l2g), full(l2b)],
        out_specs=tok(d),
        out_shape=jax.ShapeDtypeStruct((bsz, s, d), F32),
        compiler_params=pltpu.CompilerParams(
            dimension_semantics=("arbitrary", "arbitrary"), vmem_limit_bytes=VMEM_LIMIT),
        name="post",
    )(x, attn, conv, g1, sh2, sc2, g2, w_out, l1g, l1b, w_gate, w_up, w_down, l2g, l2b)


def _pack_weights(w_in):
    n_front = OFF_WI + IDX_HEADS
    front = jnp.pad(w_in[:, :n_front], ((0, 0), (0, FRONT - n_front)))
    return jnp.concatenate([front, w_in[:, n_front:]], axis=1).astype(BF16)


def _rope_tables(s):
    pos = jnp.arange(s, dtype=F32)
    inv = ROPE_THETA ** (-(jnp.arange(0, ROPE_DIM, 2, dtype=F32) / ROPE_DIM))
    ang = pos[:, None] * inv[None, :]
    cos, sin = jnp.cos(ang), jnp.sin(ang)
    half = ROPE_DIM // 2

    def region(n_heads, dim, rope, scale):
        if rope:
            own = jnp.concatenate([cos, cos, jnp.ones((s, dim - ROPE_DIM), F32)], axis=1)
            up = jnp.concatenate([-sin, jnp.zeros((s, dim - half), F32)], axis=1)
            dn = jnp.concatenate([jnp.zeros((s, half), F32), sin,
                                  jnp.zeros((s, dim - ROPE_DIM), F32)], axis=1)
        else:
            own = jnp.ones((s, dim), F32)
            up = dn = jnp.zeros((s, dim), F32)
        return tuple(jnp.tile(t, (1, n_heads)) * scale for t in (own, up, dn))

    regions = [
        region(LANES // HEAD_DIM, HEAD_DIM, True, HEAD_DIM ** -0.5 * LOG2E),
        region(1, HEAD_DIM, True, 1.0),
        region(1, HEAD_DIM, False, 1.0),
        region(LANES // IDX_DIM, IDX_DIM, True, 1.0),
        region(1, IDX_DIM, True, 1.0),
        region(1, IDX_HEADS, False, IDX_HEADS ** -0.5 * IDX_DIM ** -0.5),
        region(1, FRONT - OFF_WI - IDX_HEADS, False, 1.0),
    ]
    return tuple(jnp.concatenate([r[t] for r in regions], axis=1) for t in range(3))


def kernel(x, c, w_ada, b_ada, w_in, conv_w, attn_norm_g, conv_norm_g, w_out, ln1_g, ln1_b,
           w_gate, w_up, w_down, ln2_g, ln2_b):
    bsz, s, d = x.shape
    ctab, sup, sdn = _rope_tables(s)
    grp = CONV_WIDTH // CONV_GROUPS
    gmat = jnp.asarray(np.kron(np.eye(CONV_GROUPS), np.ones((grp, grp))), BF16)
    for l in range(DEPTH):
        mod = _ada(c, w_ada[l], b_ada[l])
        sh1, sc1, g1, sh2, sc2, g2 = [m.reshape(bsz, 1, d) for m in jnp.split(mod, 6, axis=-1)]
        q, kv, qi, ki, vt, wt, conv = _inproj(
            x, sh1, sc1, _pack_weights(w_in[l]), ctab, sup, sdn, conv_w[l],
            conv_norm_g[l].reshape(1, -1), gmat)
        head_sum = jnp.asarray(np.kron(np.eye(N_HEADS), np.ones((1, HEAD_DIM))), BF16)
        attn = _attention(q, qi, wt, kv, ki, vt, attn_norm_g[l].reshape(1, -1), head_sum)
        x = _post(x, attn, conv, g1, sh2, sc2, g2,
                  w_out[l].astype(BF16), ln1_g[l].reshape(1, -1), ln1_b[l].reshape(1, -1),
                  w_gate[l].astype(BF16), w_up[l].astype(BF16), w_down[l].astype(BF16),
                  ln2_g[l].reshape(1, -1), ln2_b[l].reshape(1, -1))
    return x
```

```python
import functools

import numpy as np
import jax
import jax.numpy as jnp
from jax import lax
from jax.experimental import pallas as pl
from jax.experimental.pallas import tpu as pltpu

D_MODEL = 1024
N_HEADS = 8
HEAD_DIM = 64
ATTN_WIDTH = N_HEADS * HEAD_DIM
IDX_HEADS = 8
IDX_DIM = 32
TOPK_MAX = 256
CONV_WIDTH = D_MODEL - ATTN_WIDTH
CONV_GROUPS = 8
CONV_K = 3
ROPE_THETA = 500000.0
ROPE_DIM = HEAD_DIM // 4
D_FF = 2816
DEPTH = 1
DEEPNORM_ALPHA = (2.0 * DEPTH) ** 0.25
EPS = 1e-5
NEG_INF = -1e30

F32 = jnp.float32
BF16 = jnp.bfloat16
I32 = jnp.int32

FRONT = 1024
OFF_Q, OFF_K, OFF_V, OFF_QI, OFF_KI, OFF_WI = 0, 512, 576, 640, 896, 928
N_COLS = FRONT + 3 * CONV_WIDTH
ROPE_PATTERN = (0, 0, 0, 0, 1, 2, 2, 3)

TM_PROJ = 512
TQ = 512
PV_HEADS = 4
KC = 512
CNT_SLAB = 64
CNT_ROWS = 32
VT_ROWS = 80
LOG2E = 1.4426950408889634
BOUND_SLACK = 1.002
MIN_DENOM = 2.0 ** -60
LANES = 128
VMEM_LIMIT = 56 * 1024 * 1024


def _key_to_float(key):
    bits = key ^ (lax.shift_right_arithmetic(key, 31) & jnp.int32(0x7FFFFFFF))
    return lax.bitcast_convert_type(bits, F32)


def _dot(a, b):
    return jnp.dot(a, b, preferred_element_type=F32)


def _dot_nt(a, b):
    return lax.dot_general(a, b, (((1,), (1,)), ((), ())), preferred_element_type=F32)


def _split_bf16(a):
    hi = a.astype(BF16)
    lo = (a - hi.astype(F32)).astype(BF16)
    return hi, lo


def _ada_kernel(c_ref, w_ref, b_ref, o_ref):
    c = c_ref[...]
    ca = c * (1.0 / (1.0 + jnp.exp(-c)))
    ca_hi, ca_lo = _split_bf16(ca)
    w_hi, w_lo = _split_bf16(w_ref[...])
    acc = _dot(ca_hi, w_hi) + _dot(ca_hi, w_lo) + _dot(ca_lo, w_hi)
    o_ref[...] = acc + b_ref[...]


def _ada(c, w_ada, b_ada):
    bsz, d = c.shape
    n = w_ada.shape[1]
    blk = 1024
    return pl.pallas_call(
        _ada_kernel,
        grid=(n // blk,),
        in_specs=[
            pl.BlockSpec((bsz, d), lambda j: (0, 0)),
            pl.BlockSpec((d, blk), lambda j: (0, j)),
            pl.BlockSpec((1, blk), lambda j: (0, j)),
        ],
        out_specs=pl.BlockSpec((bsz, blk), lambda j: (0, j)),
        out_shape=jax.ShapeDtypeStruct((bsz, n), F32),
        name="ada",
    )(c, w_ada, b_ada.reshape(1, n))


def _inproj_kernel(x_ref, sh_ref, sc_ref, w_ref, ct_ref, su_ref, sd_ref, cw_ref, cg_ref, gm_ref,
                   q_ref, kv_ref, qi_ref, ki_ref, vt_ref, wt_ref, cv_ref, ubuf, halo, *, tm):
    i = pl.program_id(0)
    b = pl.program_id(1)
    h = (x_ref[0] * (1.0 + sc_ref[0]) + sh_ref[0]).astype(BF16)

    front = _dot(h, w_ref[:, 0:FRONT])
    blocks = []
    for blk in range(FRONT // LANES):
        lanes = slice(blk * LANES, (blk + 1) * LANES)
        f = front[:, lanes]
        up = pltpu.roll(f, LANES - ROPE_DIM // 2, axis=1)
        dn = pltpu.roll(f, ROPE_DIM // 2, axis=1)
        pat = slice(ROPE_PATTERN[blk] * LANES, (ROPE_PATTERN[blk] + 1) * LANES)
        blocks.append(f * ct_ref[:, pat] + up * su_ref[:, pat] + dn * sd_ref[:, pat])
    roped = jnp.concatenate(blocks, axis=1)
    q_ref[0] = roped[:, OFF_Q:OFF_K].astype(BF16)
    kv = roped[:, OFF_K:OFF_QI]
    kv_ref[0] = kv.astype(BF16)
    qi_ref[0] = roped[:, OFF_QI:OFF_KI].astype(BF16)
    kw = roped[:, OFF_KI:FRONT]
    ki_ref[0] = kw.astype(BF16)
    vt_ref[0, 0:HEAD_DIM, :] = jnp.transpose(kv)[HEAD_DIM:2 * HEAD_DIM, :].astype(BF16)
    vt_ref[0, HEAD_DIM:VT_ROWS, :] = jnp.ones((VT_ROWS - HEAD_DIM, tm), BF16)
    wt_ref[0] = jnp.transpose(kw)[OFF_WI - OFF_KI:OFF_WI - OFF_KI + IDX_HEADS, :]

    cp = _dot(h, w_ref[:, FRONT:N_COLS])
    gate_b = cp[:, 0:CONV_WIDTH]
    u = cp[:, CONV_WIDTH:2 * CONV_WIDTH] * cp[:, 2 * CONV_WIDTH:3 * CONV_WIDTH]

    @pl.when(i == 0)
    def _():
        ubuf[0:8, :] = jnp.zeros((8, CONV_WIDTH), F32)

    @pl.when(i > 0)
    def _():
        ubuf[0:8, :] = halo[b]

    ubuf[8:8 + tm, :] = u
    conv = (cw_ref[0:1, :] * ubuf[6:6 + tm, :] + cw_ref[1:2, :] * ubuf[7:7 + tm, :]
            + cw_ref[2:3, :] * u)
    y = gate_b * conv
    y2_hi, y2_lo = _split_bf16(y * y)
    ssq = _dot(y2_hi, gm_ref[...]) + _dot(y2_lo, gm_ref[...])
    yn = y * lax.rsqrt(ssq * (1.0 / (CONV_WIDTH // CONV_GROUPS)) + EPS) * cg_ref[...]
    cv_ref[0] = yn.astype(BF16)
    halo[b] = ubuf[tm:tm + 8, :]


def _inproj(x, sh1, sc1, w_all, ctab, sup, sdn, conv_w, conv_g, gmat):
    bsz, s, d = x.shape
    tm = min(TM_PROJ, s)
    grid = (s // tm, bsz)
    tok = lambda w: pl.BlockSpec((1, tm, w), lambda i, b: (b, i, 0))
    tok_t = lambda r: pl.BlockSpec((1, r, tm), lambda i, b: (b, 0, i))
    vec = pl.BlockSpec((1, 1, d), lambda i, b: (b, 0, 0))
    full = lambda a: pl.BlockSpec(a.shape, lambda i, b: (0,) * a.ndim)
    table = pl.BlockSpec((tm, ctab.shape[1]), lambda i, b: (i, 0))
    return pl.pallas_call(
        functools.partial(_inproj_kernel, tm=tm),
        grid=grid,
        in_specs=[tok(d), vec, vec, full(w_all), table, table, table,
                  full(conv_w), full(conv_g), full(gmat)],
        out_specs=[tok(ATTN_WIDTH), tok(2 * HEAD_DIM), tok(IDX_HEADS * IDX_DIM),
                   tok(LANES), tok_t(VT_ROWS), tok_t(IDX_HEADS), tok(CONV_WIDTH)],
        out_shape=[
            jax.ShapeDtypeStruct((bsz, s, ATTN_WIDTH), BF16),
            jax.ShapeDtypeStruct((bsz, s, 2 * HEAD_DIM), BF16),
            jax.ShapeDtypeStruct((bsz, s, IDX_HEADS * IDX_DIM), BF16),
            jax.ShapeDtypeStruct((bsz, s, LANES), BF16),
            jax.ShapeDtypeStruct((bsz, VT_ROWS, s), BF16),
            jax.ShapeDtypeStruct((bsz, IDX_HEADS, s), F32),
            jax.ShapeDtypeStruct((bsz, s, CONV_WIDTH), BF16),
        ],
        scratch_shapes=[pltpu.VMEM((tm + 8, CONV_WIDTH), F32),
                        pltpu.VMEM((bsz, 8, CONV_WIDTH), F32)],
        compiler_params=pltpu.CompilerParams(
            dimension_semantics=("arbitrary", "arbitrary"), vmem_limit_bytes=VMEM_LIMIT),
        name="inproj",
    )(x, sh1, sc1, w_all, ctab, sup, sdn, conv_w, conv_g, gmat)


def _attn_kernel(q_ref, qi_ref, wt_ref, kv_ref, ki_ref, vt_ref, g_ref, hsum_ref, o_ref,
                 score_scr, bias_scr, p_scr, acc_scr, kmax_scr, tie_state,
                 *, s, k_sel, tq, kc):
    j = pl.program_id(1)
    t0 = j * tq
    n_chunks = (t0 + tq + kc - 1) // kc
    n_skip = (s - n_chunks * kc).astype(F32)
    kf = float(k_sel)

    key_i = lax.broadcasted_iota(I32, (kc, tq), 0)
    qry_t = t0 + lax.broadcasted_iota(I32, (kc, tq), 1)

    def for_chunks(body):
        def pair(i, carry):
            body(2 * i)
            body(2 * i + 1)
            return carry

        lax.fori_loop(0, n_chunks // 2, pair, 0)

        @pl.when(n_chunks % 2 == 1)
        def _():
            body(n_chunks - 1)

    def fold_rows(a, rows):
        return jnp.sum(a.reshape(kc // rows, rows, tq), axis=0)

    qi = qi_ref[0]
    qi_h = [qi[:, hh * IDX_DIM:(hh + 1) * IDX_DIM] for hh in range(IDX_HEADS)]
    wt = wt_ref[0]
    wt_h = [wt[hh:hh + 1, :] for hh in range(IDX_HEADS)]

    def score_chunk(c):
        off = pl.multiple_of(c * kc, kc)
        ki_c = ki_ref[0, pl.ds(off, kc), 0:IDX_DIM]
        acc = jnp.zeros((kc, tq), F32)
        for hh in range(IDX_HEADS):
            acc = acc + jnp.maximum(_dot_nt(ki_c, qi_h[hh]), 0.0) * wt_h[hh]
        score_scr[pl.ds(off, kc), :] = jnp.where(off + key_i <= qry_t, acc, NEG_INF)

    for_chunks(score_chunk)

    def count_ge(cand_f):
        def body(c, cnt):
            off = pl.multiple_of(c * kc, kc)
            for r0 in range(0, kc, CNT_SLAB):
                ind = jnp.where(score_scr[pl.ds(off + r0, CNT_SLAB), :] >= cand_f, 1.0, 0.0)
                parts = [ind[r:r + CNT_ROWS, :] for r in range(0, CNT_SLAB, CNT_ROWS)]
                while len(parts) > 1:
                    parts = [parts[i] + parts[i + 1] for i in range(0, len(parts), 2)]
                cnt = cnt + parts[0]
            return cnt

        cnt = lax.fori_loop(0, n_chunks, body, jnp.zeros((CNT_ROWS, tq), F32))
        return jnp.sum(cnt, axis=0, keepdims=True) + jnp.where(cand_f <= NEG_INF, n_skip, 0.0)

    def bit_body(it, carry):
        prefix, cge, cgt = carry
        cand = prefix + lax.shift_left(jnp.int32(1), 31 - it)
        cnt = count_ge(_key_to_float(cand))
        ok = cnt >= kf
        return jnp.where(ok, cand, prefix), jnp.where(ok, cnt, cge), jnp.where(ok, cgt, cnt)

    thr_key, cge, cgt = lax.fori_loop(
        0, 32, bit_body,
        (jnp.full((1, tq), -2 ** 31, I32), jnp.full((1, tq), float(s), F32),
         jnp.zeros((1, tq), F32)))
    thr = _key_to_float(thr_key)

    def count_keys(indicator):
        def body(c, cnt):
            off = pl.multiple_of(c * kc, kc)
            return cnt + fold_rows(indicator(score_scr[pl.ds(off, kc), :], off), CNT_ROWS)
        cnt = lax.fori_loop(0, n_chunks, body, jnp.zeros((CNT_ROWS, tq), F32))
        return jnp.sum(cnt, axis=0, keepdims=True)

    has_ties = jnp.max(cge) > kf

    @pl.when(jnp.logical_not(has_ties))
    def _():
        def body(c, carry):
            off = pl.multiple_of(c * kc, kc)
            kk = score_scr[pl.ds(off, kc), :]
            causal_bias = jnp.where(off + key_i <= qry_t, 0.0, NEG_INF)
            bias_scr[pl.ds(off, kc), :] = jnp.where(kk >= thr, causal_bias, NEG_INF)
            return carry
        lax.fori_loop(0, n_chunks, body, 0)

    @pl.when(has_ties)
    def _():
        thr_next = _key_to_float(thr_key + 1)
        need = kf - cgt
        n_bits = int(s).bit_length()

        def band_count(indicator):
            return count_keys(lambda kk, off: jnp.where(
                kk >= thr, jnp.where(kk >= thr_next, 0.0, indicator(kk - thr, off)), 0.0))

        tie_state[...] = jnp.zeros((8, tq), F32)

        def mark_equal(c, n_above):
            off = pl.multiple_of(c * kc, kc)
            kk = score_scr[pl.ds(off, kc), :]
            member = jnp.where(kk >= thr, jnp.where(kk >= thr_next, 0.0, 1.0), 0.0)
            above = jnp.where(kk - thr > 0.0, member, 0.0)
            bias_scr[pl.ds(off, kc), :] = member - above
            return n_above + fold_rows(above, CNT_ROWS)

        n_above = lax.fori_loop(0, n_chunks, mark_equal, jnp.zeros((CNT_ROWS, tq), F32))

        @pl.when(jnp.max(n_above) > 0.0)
        def _():
            def res_body(it, carry):
                prefix, cgt2 = carry
                cand = prefix + lax.shift_left(jnp.int32(1), 30 - it)
                cand_f = lax.bitcast_convert_type(cand, F32)
                cnt = band_count(lambda d, off: jnp.where(d >= cand_f, 1.0, 0.0))
                ok = cnt >= need
                return jnp.where(ok, cand, prefix), jnp.where(ok, cgt2, cnt)

            d_key, cgt2 = lax.fori_loop(0, 31, res_body,
                                        (jnp.zeros((1, tq), I32), jnp.zeros((1, tq), F32)))
            d_new = lax.bitcast_convert_type(d_key, F32)
            tie_state[0:1, :] = d_new
            tie_state[1:2, :] = cgt2

            def mark(c, carry):
                off = pl.multiple_of(c * kc, kc)
                kk = score_scr[pl.ds(off, kc), :]
                bias_scr[pl.ds(off, kc), :] = jnp.where(
                    kk >= thr,
                    jnp.where(kk >= thr_next, 0.0, jnp.where(kk - thr == d_new, 1.0, 0.0)), 0.0)
                return carry

            lax.fori_loop(0, n_chunks, mark, 0)

        d_thr = tie_state[0:1, :]
        need_eq = need - tie_state[1:2, :]
        row_f = lax.broadcasted_iota(I32, (CNT_SLAB, tq), 0).astype(F32)

        def marked_below(cand_f):
            def body(c, cnt):
                off = pl.multiple_of(c * kc, kc)
                for r0 in range(0, kc, CNT_SLAB):
                    rel = cand_f - (off + r0).astype(F32)
                    ind = jnp.where(row_f < rel, bias_scr[pl.ds(off + r0, CNT_SLAB), :], 0.0)
                    parts = [ind[r:r + CNT_ROWS, :] for r in range(0, CNT_SLAB, CNT_ROWS)]
                    while len(parts) > 1:
                        parts = [parts[i] + parts[i + 1] for i in range(0, len(parts), 2)]
                    cnt = cnt + parts[0]
                return cnt

            cnt = lax.fori_loop(0, n_chunks, body, jnp.zeros((CNT_ROWS, tq), F32))
            return jnp.sum(cnt, axis=0, keepdims=True)

        def tie_body(it, jm):
            cand = jm + lax.shift_left(jnp.int32(1), (n_bits - 1) - it)
            return jnp.where(marked_below(cand.astype(F32)) < need_eq, cand, jm)

        jm = lax.fori_loop(0, n_bits, tie_body, jnp.zeros((1, tq), I32))

        def body(c, carry):
            off = pl.multiple_of(c * kc, kc)
            kk = score_scr[pl.ds(off, kc), :]
            idx = off + key_i
            d = kk - thr
            causal_bias = jnp.where(idx <= qry_t, 0.0, NEG_INF)
            tie_bias = jnp.where(idx <= jm, causal_bias, NEG_INF)
            band_bias = jnp.where(d > d_thr, causal_bias, jnp.where(d == d_thr, tie_bias, NEG_INF))
            bias_scr[pl.ds(off, kc), :] = jnp.where(
                kk >= thr_next, causal_bias, jnp.where(kk >= thr, band_bias, NEG_INF))
            return carry
        lax.fori_loop(0, n_chunks, body, 0)

    q = q_ref[0]
    q_h = [q[:, hh * HEAD_DIM:(hh + 1) * HEAD_DIM] for hh in range(N_HEADS)]

    def masked_logits(c, hh):
        off = pl.multiple_of(c * kc, kc)
        k_c = kv_ref[0, pl.ds(off, kc), 0:HEAD_DIM]
        return _dot_nt(k_c, q_h[hh]) + bias_scr[pl.ds(off, kc), :]

    def attend(m_all):
        acc_scr[...] = jnp.zeros((VT_ROWS, N_HEADS * tq), F32)
        for g in range(N_HEADS // PV_HEADS):
            def exp_chunk(c, g=g):
                off = pl.multiple_of(c * kc, kc)
                for hl in range(PV_HEADS):
                    hh = g * PV_HEADS + hl
                    p = jnp.exp2(masked_logits(c, hh) - m_all[hh:hh + 1, :])
                    p_scr[pl.ds(off, kc), hl * tq:(hl + 1) * tq] = p.astype(BF16)

            for_chunks(exp_chunk)

            def pv_chunk(c, g=g):
                off = pl.multiple_of(c * kc, kc)
                cols = slice(g * PV_HEADS * tq, (g + 1) * PV_HEADS * tq)
                acc_scr[:, cols] += _dot(vt_ref[0, :, pl.ds(off, kc)], p_scr[pl.ds(off, kc), :])

            for_chunks(pv_chunk)

    @pl.when(j == 0)
    def _():
        kf32 = kv_ref[0][:, 0:HEAD_DIM].astype(F32)
        kmax_scr[...] = jnp.full((1, LANES), jnp.max(jnp.sum(kf32 * kf32, axis=-1, keepdims=True)), F32)

    qf32 = q.astype(F32)
    qsq_hi, qsq_lo = _split_bf16(qf32 * qf32)
    qn2 = _dot_nt(hsum_ref[...], qsq_hi) + _dot_nt(hsum_ref[...], qsq_lo)
    attend(jnp.sqrt(qn2 * kmax_scr[0:1, 0:1]) * BOUND_SLACK)

    @pl.when(jnp.min(acc_scr[HEAD_DIM:HEAD_DIM + 1, :]) < MIN_DENOM)
    def _():
        def max_body(c, mparts):
            return tuple(
                jnp.maximum(mparts[hh],
                            jnp.max(masked_logits(c, hh).reshape(kc // 8, 8, tq), axis=0))
                for hh in range(N_HEADS))

        mparts = lax.fori_loop(0, n_chunks, max_body,
                               tuple(jnp.full((8, tq), -3e38, F32) for _ in range(N_HEADS)))
        attend(jnp.concatenate([jnp.max(mp, axis=0, keepdims=True) for mp in mparts], axis=0))

    outs = []
    for hh in range(N_HEADS):
        o = (acc_scr[0:HEAD_DIM, hh * tq:(hh + 1) * tq]
             / acc_scr[HEAD_DIM:HEAD_DIM + 1, hh * tq:(hh + 1) * tq])
        ms = jnp.mean(o * o, axis=0, keepdims=True)
        outs.append(o * lax.rsqrt(ms + EPS))
    out_t = jnp.concatenate(outs, axis=0)
    o_ref[0] = (jnp.transpose(out_t) * g_ref[...]).astype(BF16)


def _attention(q, qi, wt, kv, ki, vt, attn_g, head_sum):
    bsz, s, _ = q.shape
    tq = min(TQ, s)
    kc = min(KC, s)
    k_sel = min(TOPK_MAX, s // 4)
    grid = (bsz, s // tq)
    qtile = lambda w: pl.BlockSpec((1, tq, w), lambda b, j: (b, j, 0))
    seq = lambda w: pl.BlockSpec((1, s, w), lambda b, j: (b, 0, 0))
    return pl.pallas_call(
        functools.partial(_attn_kernel, s=s, k_sel=k_sel, tq=tq, kc=kc),
        grid=grid,
        in_specs=[qtile(ATTN_WIDTH), qtile(IDX_HEADS * IDX_DIM),
                  pl.BlockSpec((1, IDX_HEADS, tq), lambda b, j: (b, 0, j)),
                  seq(2 * HEAD_DIM), seq(LANES),
                  pl.BlockSpec((1, VT_ROWS, s), lambda b, j: (b, 0, 0)),
                  pl.BlockSpec((1, ATTN_WIDTH), lambda b, j: (0, 0)),
                  pl.BlockSpec((N_HEADS, ATTN_WIDTH), lambda b, j: (0, 0))],
        out_specs=qtile(ATTN_WIDTH),
        out_shape=jax.ShapeDtypeStruct((bsz, s, ATTN_WIDTH), BF16),
        scratch_shapes=[pltpu.VMEM((s, tq), F32), pltpu.VMEM((s, tq), F32),
                        pltpu.VMEM((s, PV_HEADS * tq), BF16),
                        pltpu.VMEM((VT_ROWS, N_HEADS * tq), F32),
                        pltpu.VMEM((1, LANES), F32), pltpu.VMEM((8, tq), F32)],
        compiler_params=pltpu.CompilerParams(
            dimension_semantics=("arbitrary", "arbitrary"), vmem_limit_bytes=VMEM_LIMIT),
        name="attn",
    )(q, qi, wt, kv, ki, vt, attn_g, head_sum)


def _layer_norm(y, g, b):
    mu = jnp.mean(y, axis=-1, keepdims=True)
    yc = y - mu
    var = jnp.mean(yc * yc, axis=-1, keepdims=True)
    return yc * lax.rsqrt(var + EPS) * g + b


def _post_kernel(x_ref, at_ref, cv_ref, g1_ref, sh2_ref, sc2_ref, g2_ref,
                 wo_ref, l1g_ref, l1b_ref, wg_ref, wu_ref, wd_ref, l2g_ref, l2b_ref, o_ref):
    mix = _dot(at_ref[0], wo_ref[0:ATTN_WIDTH, :]) + _dot(cv_ref[0], wo_ref[ATTN_WIDTH:D_MODEL, :])
    x1 = _layer_norm(DEEPNORM_ALPHA * x_ref[0] + g1_ref[0] * mix, l1g_ref[...], l1b_ref[...])
    h2 = (x1 * (1.0 + sc2_ref[0]) + sh2_ref[0]).astype(BF16)
    gt = _dot(h2, wg_ref[...])
    up = _dot(h2, wu_ref[...])
    hid = (gt * (1.0 / (1.0 + jnp.exp(-gt))) * up).astype(BF16)
    ff = _dot(hid, wd_ref[...])
    o_ref[0] = _layer_norm(DEEPNORM_ALPHA * x1 + g2_ref[0] * ff, l2g_ref[...], l2b_ref[...])


def _post(x, attn, conv, g1, sh2, sc2, g2, w_out, l1g, l1b, w_gate, w_up, w_down, l2g, l2b):
    bsz, s, d = x.shape
    tm = min(TM_PROJ, s)
    grid = (bsz, s // tm)
    tok = lambda w: pl.BlockSpec((1, tm, w), lambda b, i: (b, i, 0))
    vec = pl.BlockSpec((1, 1, d), lambda b, i: (b, 0, 0))
    full = lambda a: pl.BlockSpec(a.shape, lambda b, i: (0,) * a.ndim,
                                  pipeline_mode=pl.Buffered(1))
    return pl.pallas_call(
        _post_kernel,
        grid=grid,
        in_specs=[tok(d), tok(ATTN_WIDTH), tok(CONV_WIDTH), vec, vec, vec, vec,
                  full(w_out), full(l1g), full(l1b), full(w_gate), full(w_up), full(w_down),
                  full(l2g), full(l2b)],
        out_specs=tok(d),
        out_shape=jax.ShapeDtypeStruct((bsz, s, d), F32),
        compiler_params=pltpu.CompilerParams(
            dimension_semantics=("arbitrary", "arbitrary"), vmem_limit_bytes=VMEM_LIMIT),
        name="post",
    )(x, attn, conv, g1, sh2, sc2, g2, w_out, l1g, l1b, w_gate, w_up, w_down, l2g, l2b)


def _pack_weights(w_in):
    n_front = OFF_WI + IDX_HEADS
    front = jnp.pad(w_in[:, :n_front], ((0, 0), (0, FRONT - n_front)))
    return jnp.concatenate([front, w_in[:, n_front:]], axis=1).astype(BF16)


def _rope_tables(s):
    pos = jnp.arange(s, dtype=F32)
    inv = ROPE_THETA ** (-(jnp.arange(0, ROPE_DIM, 2, dtype=F32) / ROPE_DIM))
    ang = pos[:, None] * inv[None, :]
    cos, sin = jnp.cos(ang), jnp.sin(ang)
    half = ROPE_DIM // 2

    def region(n_heads, dim, rope, scale):
        if rope:
            own = jnp.concatenate([cos, cos, jnp.ones((s, dim - ROPE_DIM), F32)], axis=1)
            up = jnp.concatenate([-sin, jnp.zeros((s, dim - half), F32)], axis=1)
            dn = jnp.concatenate([jnp.zeros((s, half), F32), sin,
                                  jnp.zeros((s, dim - ROPE_DIM), F32)], axis=1)
        else:
            own = jnp.ones((s, dim), F32)
            up = dn = jnp.zeros((s, dim), F32)
        return tuple(jnp.tile(t, (1, n_heads)) * scale for t in (own, up, dn))

    regions = [
        region(LANES // HEAD_DIM, HEAD_DIM, True, HEAD_DIM ** -0.5 * LOG2E),
        region(1, HEAD_DIM, True, 1.0),
        region(1, HEAD_DIM, False, 1.0),
        region(LANES // IDX_DIM, IDX_DIM, True, 1.0),
        region(1, IDX_DIM, True, 1.0),
        region(1, IDX_HEADS, False, IDX_HEADS ** -0.5 * IDX_DIM ** -0.5),
        region(1, FRONT - OFF_WI - IDX_HEADS, False, 1.0),
    ]
    return tuple(jnp.concatenate([r[t] for r in regions], axis=1) for t in range(3))


def kernel(x, c, w_ada, b_ada, w_in, conv_w, attn_norm_g, conv_norm_g, w_out, ln1_g, ln1_b,
           w_gate, w_up, w_down, ln2_g, ln2_b):
    bsz, s, d = x.shape
    ctab, sup, sdn = _rope_tables(s)
    grp = CONV_WIDTH // CONV_GROUPS
    gmat = jnp.asarray(np.kron(np.eye(CONV_GROUPS), np.ones((grp, grp))), BF16)
    for l in range(DEPTH):
        mod = _ada(c, w_ada[l], b_ada[l])
        sh1, sc1, g1, sh2, sc2, g2 = [m.reshape(bsz, 1, d) for m in jnp.split(mod, 6, axis=-1)]
        q, kv, qi, ki, vt, wt, conv = _inproj(
            x, sh1, sc1, _pack_weights(w_in[l]), ctab, sup, sdn, conv_w[l],
            conv_norm_g[l].reshape(1, -1), gmat)
        head_sum = jnp.asarray(np.kron(np.eye(N_HEADS), np.ones((1, HEAD_DIM))), BF16)
        attn = _attention(q, qi, wt, kv, ki, vt, attn_norm_g[l].reshape(1, -1), head_sum)
        x = _post(x, attn, conv, g1, sh2, sc2, g2,
                  w_out[l].astype(BF16), ln1_g[l].reshape(1, -1), ln1_b[l].reshape(1, -1),
                  w_gate[l].astype(BF16), w_up[l].astype(BF16), w_down[l].astype(BF16),
                  ln2_g[l].reshape(1, -1), ln2_b[l].reshape(1, -1))
    return x
```

```python
import functools

import numpy as np
import jax
import jax.numpy as jnp
from jax import lax
from jax.experimental import pallas as pl
from jax.experimental.pallas import tpu as pltpu

D_MODEL = 1024
N_HEADS = 8
HEAD_DIM = 64
ATTN_WIDTH = N_HEADS * HEAD_DIM
IDX_HEADS = 8
IDX_DIM = 32
TOPK_MAX = 256
CONV_WIDTH = D_MODEL - ATTN_WIDTH
CONV_GROUPS = 8
CONV_K = 3
ROPE_THETA = 500000.0
ROPE_DIM = HEAD_DIM // 4
D_FF = 2816
DEPTH = 1
DEEPNORM_ALPHA = (2.0 * DEPTH) ** 0.25
EPS = 1e-5
NEG_INF = -1e30

F32 = jnp.float32
BF16 = jnp.bfloat16
I32 = jnp.int32

FRONT = 1024
OFF_Q, OFF_K, OFF_V, OFF_QI, OFF_KI, OFF_WI = 0, 512, 576, 640, 896, 928
N_COLS = FRONT + 3 * CONV_WIDTH
ROPE_PATTERN = (0, 0, 0, 0, 1, 2, 2, 3)

TM_PROJ = 512
TQ = 512
PV_HEADS = 4
KC = 512
CNT_SLAB = 64
CNT_ROWS = 32
VT_ROWS = 80
LOG2E = 1.4426950408889634
BOUND_SLACK = 1.002
MIN_DENOM = 2.0 ** -60
LANES = 128
VMEM_LIMIT = 56 * 1024 * 1024


def _key_to_float(key):
    bits = key ^ (lax.shift_right_arithmetic(key, 31) & jnp.int32(0x7FFFFFFF))
    return lax.bitcast_convert_type(bits, F32)


def _dot(a, b):
    return jnp.dot(a, b, preferred_element_type=F32)


def _dot_nt(a, b):
    return lax.dot_general(a, b, (((1,), (1,)), ((), ())), preferred_element_type=F32)


def _split_bf16(a):
    hi = a.astype(BF16)
    lo = (a - hi.astype(F32)).astype(BF16)
    return hi, lo


def _ada_kernel(c_ref, w_ref, b_ref, o_ref):
    c = c_ref[...]
    ca = c * (1.0 / (1.0 + jnp.exp(-c)))
    ca_hi, ca_lo = _split_bf16(ca)
    w_hi, w_lo = _split_bf16(w_ref[...])
    acc = _dot(ca_hi, w_hi) + _dot(ca_hi, w_lo) + _dot(ca_lo, w_hi)
    o_ref[...] = acc + b_ref[...]


def _ada(c, w_ada, b_ada):
    bsz, d = c.shape
    n = w_ada.shape[1]
    blk = 1024
    return pl.pallas_call(
        _ada_kernel,
        grid=(n // blk,),
        in_specs=[
            pl.BlockSpec((bsz, d), lambda j: (0, 0)),
            pl.BlockSpec((d, blk), lambda j: (0, j)),
            pl.BlockSpec((1, blk), lambda j: (0, j)),
        ],
        out_specs=pl.BlockSpec((bsz, blk), lambda j: (0, j)),
        out_shape=jax.ShapeDtypeStruct((bsz, n), F32),
        name="ada",
    )(c, w_ada, b_ada.reshape(1, n))


def _inproj_kernel(x_ref, sh_ref, sc_ref, w_ref, ct_ref, su_ref, sd_ref, cw_ref, cg_ref, gm_ref,
                   q_ref, kv_ref, qi_ref, ki_ref, vt_ref, wt_ref, cv_ref, ubuf, halo, *, tm):
    i = pl.program_id(0)
    b = pl.program_id(1)
    h = (x_ref[0] * (1.0 + sc_ref[0]) + sh_ref[0]).astype(BF16)

    front = _dot(h, w_ref[:, 0:FRONT])
    blocks = []
    for blk in range(FRONT // LANES):
        lanes = slice(blk * LANES, (blk + 1) * LANES)
        f = front[:, lanes]
        up = pltpu.roll(f, LANES - ROPE_DIM // 2, axis=1)
        dn = pltpu.roll(f, ROPE_DIM // 2, axis=1)
        pat = slice(ROPE_PATTERN[blk] * LANES, (ROPE_PATTERN[blk] + 1) * LANES)
        blocks.append(f * ct_ref[:, pat] + up * su_ref[:, pat] + dn * sd_ref[:, pat])
    roped = jnp.concatenate(blocks, axis=1)
    q_ref[0] = roped[:, OFF_Q:OFF_K].astype(BF16)
    kv = roped[:, OFF_K:OFF_QI]
    kv_ref[0] = kv.astype(BF16)
    qi_ref[0] = roped[:, OFF_QI:OFF_KI].astype(BF16)
    kw = roped[:, OFF_KI:FRONT]
    ki_ref[0] = kw.astype(BF16)
    vt_ref[0, 0:HEAD_DIM, :] = jnp.transpose(kv)[HEAD_DIM:2 * HEAD_DIM, :].astype(BF16)
    vt_ref[0, HEAD_DIM:VT_ROWS, :] = jnp.ones((VT_ROWS - HEAD_DIM, tm), BF16)
    wt_ref[0] = jnp.transpose(kw)[OFF_WI - OFF_KI:OFF_WI - OFF_KI + IDX_HEADS, :]

    cp = _dot(h, w_ref[:, FRONT:N_COLS])
    gate_b = cp[:, 0:CONV_WIDTH]
    u = cp[:, CONV_WIDTH:2 * CONV_WIDTH] * cp[:, 2 * CONV_WIDTH:3 * CONV_WIDTH]

    @pl.when(i == 0)
    def _():
        ubuf[0:8, :] = jnp.zeros((8, CONV_WIDTH), F32)

    @pl.when(i > 0)
    def _():
        ubuf[0:8, :] = halo[b]

    ubuf[8:8 + tm, :] = u
    conv = (cw_ref[0:1, :] * ubuf[6:6 + tm, :] + cw_ref[1:2, :] * ubuf[7:7 + tm, :]
            + cw_ref[2:3, :] * u)
    y = gate_b * conv
    y2_hi, y2_lo = _split_bf16(y * y)
    ssq = _dot(y2_hi, gm_ref[...]) + _dot(y2_lo, gm_ref[...])
    yn = y * lax.rsqrt(ssq * (1.0 / (CONV_WIDTH // CONV_GROUPS)) + EPS) * cg_ref[...]
    cv_ref[0] = yn.astype(BF16)
    halo[b] = ubuf[tm:tm + 8, :]


def _inproj(x, sh1, sc1, w_all, ctab, sup, sdn, conv_w, conv_g, gmat):
    bsz, s, d = x.shape
    tm = min(TM_PROJ, s)
    grid = (s // tm, bsz)
    tok = lambda w: pl.BlockSpec((1, tm, w), lambda i, b: (b, i, 0))
    tok_t = lambda r: pl.BlockSpec((1, r, tm), lambda i, b: (b, 0, i))
    vec = pl.BlockSpec((1, 1, d), lambda i, b: (b, 0, 0))
    full = lambda a: pl.BlockSpec(a.shape, lambda i, b: (0,) * a.ndim)
    table = pl.BlockSpec((tm, ctab.shape[1]), lambda i, b: (i, 0))
    return pl.pallas_call(
        functools.partial(_inproj_kernel, tm=tm),
        grid=grid,
        in_specs=[tok(d), vec, vec, full(w_all), table, table, table,
                  full(conv_w), full(conv_g), full(gmat)],
        out_specs=[tok(ATTN_WIDTH), tok(2 * HEAD_DIM), tok(IDX_HEADS * IDX_DIM),
                   tok(LANES), tok_t(VT_ROWS), tok_t(IDX_HEADS), tok(CONV_WIDTH)],
        out_shape=[
            jax.ShapeDtypeStruct((bsz, s, ATTN_WIDTH), BF16),
            jax.ShapeDtypeStruct((bsz, s, 2 * HEAD_DIM), BF16),
            jax.ShapeDtypeStruct((bsz, s, IDX_HEADS * IDX_DIM), BF16),
            jax.ShapeDtypeStruct((bsz, s, LANES), BF16),
            jax.ShapeDtypeStruct((bsz, VT_ROWS, s), BF16),
            jax.ShapeDtypeStruct((bsz, IDX_HEADS, s), F32),
            jax.ShapeDtypeStruct((bsz, s, CONV_WIDTH), BF16),
        ],
        scratch_shapes=[pltpu.VMEM((tm + 8, CONV_WIDTH), F32),
                        pltpu.VMEM((bsz, 8, CONV_WIDTH), F32)],
        compiler_params=pltpu.CompilerParams(
            dimension_semantics=("arbitrary", "arbitrary"), vmem_limit_bytes=VMEM_LIMIT),
        name="inproj",
    )(x, sh1, sc1, w_all, ctab, sup, sdn, conv_w, conv_g, gmat)


def _attn_kernel(q_ref, qi_ref, wt_ref, kv_ref, ki_ref, vt_ref, g_ref, hsum_ref, o_ref,
                 score_scr, bias_scr, p_scr, acc_scr, kmax_scr, tie_state,
                 *, s, k_sel, tq, kc):
    j = pl.program_id(1)
    t0 = j * tq
    n_chunks = (t0 + tq + kc - 1) // kc
    n_skip = (s - n_chunks * kc).astype(F32)
    kf = float(k_sel)

    key_i = lax.broadcasted_iota(I32, (kc, tq), 0)
    qry_t = t0 + lax.broadcasted_iota(I32, (kc, tq), 1)

    def for_chunks(body):
        def pair(i, carry):
            body(2 * i)
            body(2 * i + 1)
            return carry

        lax.fori_loop(0, n_chunks // 2, pair, 0)

        @pl.when(n_chunks % 2 == 1)
        def _():
            body(n_chunks - 1)

    def fold_rows(a, rows):
        return jnp.sum(a.reshape(kc // rows, rows, tq), axis=0)

    qi = qi_ref[0]
    qi_h = [qi[:, hh * IDX_DIM:(hh + 1) * IDX_DIM] for hh in range(IDX_HEADS)]
    wt = wt_ref[0]
    wt_h = [wt[hh:hh + 1, :] for hh in range(IDX_HEADS)]

    def score_chunk(c):
        off = pl.multiple_of(c * kc, kc)
        ki_c = ki_ref[0, pl.ds(off, kc), 0:IDX_DIM]
        acc = jnp.zeros((kc, tq), F32)
        for hh in range(IDX_HEADS):
            acc = acc + jnp.maximum(_dot_nt(ki_c, qi_h[hh]), 0.0) * wt_h[hh]
        score_scr[pl.ds(off, kc), :] = jnp.where(off + key_i <= qry_t, acc, NEG_INF)

    for_chunks(score_chunk)

    def count_ge(cand_f):
        def body(c, cnt):
            off = pl.multiple_of(c * kc, kc)
            for r0 in range(0, kc, CNT_SLAB):
                ind = jnp.where(score_scr[pl.ds(off + r0, CNT_SLAB), :] >= cand_f, 1.0, 0.0)
                parts = [ind[r:r + CNT_ROWS, :] for r in range(0, CNT_SLAB, CNT_ROWS)]
                while len(parts) > 1:
                    parts = [parts[i] + parts[i + 1] for i in range(0, len(parts), 2)]
                cnt = cnt + parts[0]
            return cnt

        cnt = lax.fori_loop(0, n_chunks, body, jnp.zeros((CNT_ROWS, tq), F32))
        return jnp.sum(cnt, axis=0, keepdims=True) + jnp.where(cand_f <= NEG_INF, n_skip, 0.0)

    def bit_body(it, carry):
        prefix, cge, cgt = carry
        cand = prefix + lax.shift_left(jnp.int32(1), 31 - it)
        cnt = count_ge(_key_to_float(cand))
        ok = cnt >= kf
        return jnp.where(ok, cand, prefix), jnp.where(ok, cnt, cge), jnp.where(ok, cgt, cnt)

    thr_key, cge, cgt = lax.fori_loop(
        0, 32, bit_body,
        (jnp.full((1, tq), -2 ** 31, I32), jnp.full((1, tq), float(s), F32),
         jnp.zeros((1, tq), F32)))
    thr = _key_to_float(thr_key)

    def count_keys(indicator):
        def body(c, cnt):
            off = pl.multiple_of(c * kc, kc)
            return cnt + fold_rows(indicator(score_scr[pl.ds(off, kc), :], off), CNT_ROWS)
        cnt = lax.fori_loop(0, n_chunks, body, jnp.zeros((CNT_ROWS, tq), F32))
        return jnp.sum(cnt, axis=0, keepdims=True)

    has_ties = jnp.max(cge) > kf

    @pl.when(jnp.logical_not(has_ties))
    def _():
        def body(c, carry):
            off = pl.multiple_of(c * kc, kc)
            kk = score_scr[pl.ds(off, kc), :]
            causal_bias = jnp.where(off + key_i <= qry_t, 0.0, NEG_INF)
            bias_scr[pl.ds(off, kc), :] = jnp.where(kk >= thr, causal_bias, NEG_INF)
            return carry
        lax.fori_loop(0, n_chunks, body, 0)

    @pl.when(has_ties)
    def _():
        thr_next = _key_to_float(thr_key + 1)
        need = kf - cgt
        n_bits = int(s).bit_length()

        def band_count(indicator):
            return count_keys(lambda kk, off: jnp.where(
                kk >= thr, jnp.where(kk >= thr_next, 0.0, indicator(kk - thr, off)), 0.0))

        tie_state[...] = jnp.zeros((8, tq), F32)

        def mark_equal(c, n_above):
            off = pl.multiple_of(c * kc, kc)
            kk = score_scr[pl.ds(off, kc), :]
            member = jnp.where(kk >= thr, jnp.where(kk >= thr_next, 0.0, 1.0), 0.0)
            above = jnp.where(kk - thr > 0.0, member, 0.0)
            bias_scr[pl.ds(off, kc), :] = member - above
            return n_above + fold_rows(above, CNT_ROWS)

        n_above = lax.fori_loop(0, n_chunks, mark_equal, jnp.zeros((CNT_ROWS, tq), F32))

        @pl.when(jnp.max(n_above) > 0.0)
        def _():
            def res_body(it, carry):
                prefix, cgt2 = carry
                cand = prefix + lax.shift_left(jnp.int32(1), 30 - it)
                cand_f = lax.bitcast_convert_type(cand, F32)
                cnt = band_count(lambda d, off: jnp.where(d >= cand_f, 1.0, 0.0))
                ok = cnt >= need
                return jnp.where(ok, cand, prefix), jnp.where(ok, cgt2, cnt)

            d_key, cgt2 = lax.fori_loop(0, 31, res_body,
                                        (jnp.zeros((1, tq), I32), jnp.zeros((1, tq), F32)))
            d_new = lax.bitcast_convert_type(d_key, F32)
            tie_state[0:1, :] = d_new
            tie_state[1:2, :] = cgt2

            def mark(c, carry):
                off = pl.multiple_of(c * kc, kc)
                kk = score_scr[pl.ds(off, kc), :]
                bias_scr[pl.ds(off, kc), :] = jnp.where(
                    kk >= thr,
                    jnp.where(kk >= thr_next, 0.0, jnp.where(kk - thr == d_new, 1.0, 0.0)), 0.0)
                return carry

            lax.fori_loop(0, n_chunks, mark, 0)

        d_thr = tie_state[0:1, :]
        need_eq = need - tie_state[1:2, :]
        tri = jnp.where(lax.broadcasted_iota(I32, (kc, kc), 0) >= lax.broadcasted_iota(I32, (kc, kc), 1),
                        1.0, 0.0).astype(BF16)

        def body(c, seen):
            off = pl.multiple_of(c * kc, kc)
            kk = score_scr[pl.ds(off, kc), :]
            marks = bias_scr[pl.ds(off, kc), :]
            upto = _dot(tri, marks.astype(BF16)) + seen
            d = kk - thr
            causal_bias = jnp.where(off + key_i <= qry_t, 0.0, NEG_INF)
            tie_bias = jnp.where(marks > 0.0, jnp.where(upto <= need_eq, causal_bias, NEG_INF),
                                 NEG_INF)
            band_bias = jnp.where(d > d_thr, causal_bias, tie_bias)
            bias_scr[pl.ds(off, kc), :] = jnp.where(
                kk >= thr_next, causal_bias, jnp.where(kk >= thr, band_bias, NEG_INF))
            return upto[kc - 1:kc, :]

        lax.fori_loop(0, n_chunks, body, jnp.zeros((1, tq), F32))

    q = q_ref[0]
    q_h = [q[:, hh * HEAD_DIM:(hh + 1) * HEAD_DIM] for hh in range(N_HEADS)]

    def masked_logits(c, hh):
        off = pl.multiple_of(c * kc, kc)
        k_c = kv_ref[0, pl.ds(off, kc), 0:HEAD_DIM]
        return _dot_nt(k_c, q_h[hh]) + bias_scr[pl.ds(off, kc), :]

    def attend(m_all):
        acc_scr[...] = jnp.zeros((VT_ROWS, N_HEADS * tq), F32)
        for g in range(N_HEADS // PV_HEADS):
            def exp_chunk(c, g=g):
                off = pl.multiple_of(c * kc, kc)
                for hl in range(PV_HEADS):
                    hh = g * PV_HEADS + hl
                    p = jnp.exp2(masked_logits(c, hh) - m_all[hh:hh + 1, :])
                    p_scr[pl.ds(off, kc), hl * tq:(hl + 1) * tq] = p.astype(BF16)

            for_chunks(exp_chunk)

            def pv_chunk(c, g=g):
                off = pl.multiple_of(c * kc, kc)
                cols = slice(g * PV_HEADS * tq, (g + 1) * PV_HEADS * tq)
                acc_scr[:, cols] += _dot(vt_ref[0, :, pl.ds(off, kc)], p_scr[pl.ds(off, kc), :])

            for_chunks(pv_chunk)

    @pl.when(j == 0)
    def _():
        kf32 = kv_ref[0][:, 0:HEAD_DIM].astype(F32)
        kmax_scr[...] = jnp.full((1, LANES), jnp.max(jnp.sum(kf32 * kf32, axis=-1, keepdims=True)), F32)

    qf32 = q.astype(F32)
    qsq_hi, qsq_lo = _split_bf16(qf32 * qf32)
    qn2 = _dot_nt(hsum_ref[...], qsq_hi) + _dot_nt(hsum_ref[...], qsq_lo)
    attend(jnp.sqrt(qn2 * kmax_scr[0:1, 0:1]) * BOUND_SLACK)

    @pl.when(jnp.min(acc_scr[HEAD_DIM:HEAD_DIM + 1, :]) < MIN_DENOM)
    def _():
        def max_body(c, mparts):
            return tuple(
                jnp.maximum(mparts[hh],
                            jnp.max(masked_logits(c, hh).reshape(kc // 8, 8, tq), axis=0))
                for hh in range(N_HEADS))

        mparts = lax.fori_loop(0, n_chunks, max_body,
                               tuple(jnp.full((8, tq), -3e38, F32) for _ in range(N_HEADS)))
        attend(jnp.concatenate([jnp.max(mp, axis=0, keepdims=True) for mp in mparts], axis=0))

    outs = []
    for hh in range(N_HEADS):
        o = (acc_scr[0:HEAD_DIM, hh * tq:(hh + 1) * tq]
             / acc_scr[HEAD_DIM:HEAD_DIM + 1, hh * tq:(hh + 1) * tq])
        ms = jnp.mean(o * o, axis=0, keepdims=True)
        outs.append(o * lax.rsqrt(ms + EPS))
    out_t = jnp.concatenate(outs, axis=0)
    o_ref[0] = (jnp.transpose(out_t) * g_ref[...]).astype(BF16)


def _attention(q, qi, wt, kv, ki, vt, attn_g, head_sum):
    bsz, s, _ = q.shape
    tq = min(TQ, s)
    kc = min(KC, s)
    k_sel = min(TOPK_MAX, s // 4)
    grid = (bsz, s // tq)
    qtile = lambda w: pl.BlockSpec((1, tq, w), lambda b, j: (b, j, 0))
    seq = lambda w: pl.BlockSpec((1, s, w), lambda b, j: (b, 0, 0))
    return pl.pallas_call(
        functools.partial(_attn_kernel, s=s, k_sel=k_sel, tq=tq, kc=kc),
        grid=grid,
        in_specs=[qtile(ATTN_WIDTH), qtile(IDX_HEADS * IDX_DIM),
                  pl.BlockSpec((1, IDX_HEADS, tq), lambda b, j: (b, 0, j)),
                  seq(2 * HEAD_DIM), seq(LANES),
                  pl.BlockSpec((1, VT_ROWS, s), lambda b, j: (b, 0, 0)),
                  pl.BlockSpec((1, ATTN_WIDTH), lambda b, j: (0, 0)),
                  pl.BlockSpec((N_HEADS, ATTN_WIDTH), lambda b, j: (0, 0))],
        out_specs=qtile(ATTN_WIDTH),
        out_shape=jax.ShapeDtypeStruct((bsz, s, ATTN_WIDTH), BF16),
        scratch_shapes=[pltpu.VMEM((s, tq), F32), pltpu.VMEM((s, tq), F32),
                        pltpu.VMEM((s, PV_HEADS * tq), BF16),
                        pltpu.VMEM((VT_ROWS, N_HEADS * tq), F32),
                        pltpu.VMEM((1, LANES), F32), pltpu.VMEM((8, tq), F32)],
        compiler_params=pltpu.CompilerParams(
            dimension_semantics=("arbitrary", "arbitrary"), vmem_limit_bytes=VMEM_LIMIT),
        name="attn",
    )(q, qi, wt, kv, ki, vt, attn_g, head_sum)


def _layer_norm(y, g, b):
    mu = jnp.mean(y, axis=-1, keepdims=True)
    yc = y - mu
    var = jnp.mean(yc * yc, axis=-1, keepdims=True)
    return yc * lax.rsqrt(var + EPS) * g + b


def _post_kernel(x_ref, at_ref, cv_ref, g1_ref, sh2_ref, sc2_ref, g2_ref,
                 wo_ref, l1g_ref, l1b_ref, wg_ref, wu_ref, wd_ref, l2g_ref, l2b_ref, o_ref):
    mix = _dot(at_ref[0], wo_ref[0:ATTN_WIDTH, :]) + _dot(cv_ref[0], wo_ref[ATTN_WIDTH:D_MODEL, :])
    x1 = _layer_norm(DEEPNORM_ALPHA * x_ref[0] + g1_ref[0] * mix, l1g_ref[...], l1b_ref[...])
    h2 = (x1 * (1.0 + sc2_ref[0]) + sh2_ref[0]).astype(BF16)
    gt = _dot(h2, wg_ref[...])
    up = _dot(h2, wu_ref[...])
    hid = (gt * (1.0 / (1.0 + jnp.exp(-gt))) * up).astype(BF16)
    ff = _dot(hid, wd_ref[...])
    o_ref[0] = _layer_norm(DEEPNORM_ALPHA * x1 + g2_ref[0] * ff, l2g_ref[...], l2b_ref[...])


def _post(x, attn, conv, g1, sh2, sc2, g2, w_out, l1g, l1b, w_gate, w_up, w_down, l2g, l2b):
    bsz, s, d = x.shape
    tm = min(TM_PROJ, s)
    grid = (bsz, s // tm)
    tok = lambda w: pl.BlockSpec((1, tm, w), lambda b, i: (b, i, 0))
    vec = pl.BlockSpec((1, 1, d), lambda b, i: (b, 0, 0))
    full = lambda a: pl.BlockSpec(a.shape, lambda b, i: (0,) * a.ndim,
                                  pipeline_mode=pl.Buffered(1))
    return pl.pallas_call(
        _post_kernel,
        grid=grid,
        in_specs=[tok(d), tok(ATTN_WIDTH), tok(CONV_WIDTH), vec, vec, vec, vec,
                  full(w_out), full(l1g), full(l1b), full(w_gate), full(w_up), full(w_down),
                  full(l2g), full(l2b)],
        out_specs=tok(d),
        out_shape=jax.ShapeDtypeStruct((bsz, s, d), F32),
        compiler_params=pltpu.CompilerParams(
            dimension_semantics=("arbitrary", "arbitrary"), vmem_limit_bytes=VMEM_LIMIT),
        name="post",
    )(x, attn, conv, g1, sh2, sc2, g2, w_out, l1g, l1b, w_gate, w_up, w_down, l2g, l2b)


def _pack_weights(w_in):
    n_front = OFF_WI + IDX_HEADS
    front = jnp.pad(w_in[:, :n_front], ((0, 0), (0, FRONT - n_front)))
    return jnp.concatenate([front, w_in[:, n_front:]], axis=1).astype(BF16)


def _rope_tables(s):
    pos = jnp.arange(s, dtype=F32)
    inv = ROPE_THETA ** (-(jnp.arange(0, ROPE_DIM, 2, dtype=F32) / ROPE_DIM))
    ang = pos[:, None] * inv[None, :]
    cos, sin = jnp.cos(ang), jnp.sin(ang)
    half = ROPE_DIM // 2

    def region(n_heads, dim, rope, scale):
        if rope:
            own = jnp.concatenate([cos, cos, jnp.ones((s, dim - ROPE_DIM), F32)], axis=1)
            up = jnp.concatenate([-sin, jnp.zeros((s, dim - half), F32)], axis=1)
            dn = jnp.concatenate([jnp.zeros((s, half), F32), sin,
                                  jnp.zeros((s, dim - ROPE_DIM), F32)], axis=1)
        else:
            own = jnp.ones((s, dim), F32)
            up = dn = jnp.zeros((s, dim), F32)
        return tuple(jnp.tile(t, (1, n_heads)) * scale for t in (own, up, dn))

    regions = [
        region(LANES // HEAD_DIM, HEAD_DIM, True, HEAD_DIM ** -0.5 * LOG2E),
        region(1, HEAD_DIM, True, 1.0),
        region(1, HEAD_DIM, False, 1.0),
        region(LANES // IDX_DIM, IDX_DIM, True, 1.0),
        region(1, IDX_DIM, True, 1.0),
        region(1, IDX_HEADS, False, IDX_HEADS ** -0.5 * IDX_DIM ** -0.5),
        region(1, FRONT - OFF_WI - IDX_HEADS, False, 1.0),
    ]
    return tuple(jnp.concatenate([r[t] for r in regions], axis=1) for t in range(3))


def kernel(x, c, w_ada, b_ada, w_in, conv_w, attn_norm_g, conv_norm_g, w_out, ln1_g, ln1_b,
           w_gate, w_up, w_down, ln2_g, ln2_b):
    bsz, s, d = x.shape
    ctab, sup, sdn = _rope_tables(s)
    grp = CONV_WIDTH // CONV_GROUPS
    gmat = jnp.asarray(np.kron(np.eye(CONV_GROUPS), np.ones((grp, grp))), BF16)
    for l in range(DEPTH):
        mod = _ada(c, w_ada[l], b_ada[l])
        sh1, sc1, g1, sh2, sc2, g2 = [m.reshape(bsz, 1, d) for m in jnp.split(mod, 6, axis=-1)]
        q, kv, qi, ki, vt, wt, conv = _inproj(
            x, sh1, sc1, _pack_weights(w_in[l]), ctab, sup, sdn, conv_w[l],
            conv_norm_g[l].reshape(1, -1), gmat)
        head_sum = jnp.asarray(np.kron(np.eye(N_HEADS), np.ones((1, HEAD_DIM))), BF16)
        attn = _attention(q, qi, wt, kv, ki, vt, attn_norm_g[l].reshape(1, -1), head_sum)
        x = _post(x, attn, conv, g1, sh2, sc2, g2,
                  w_out[l].astype(BF16), ln1_g[l].reshape(1, -1), ln1_b[l].reshape(1, -1),
                  w_gate[l].astype(BF16), w_up[l].astype(BF16), w_down[l].astype(BF16),
                  ln2_g[l].reshape(1, -1), ln2_b[l].reshape(1, -1))
    return x
```
